```python
import math
import functools
import jax
import jax.numpy as jnp
from jax import lax
import numpy as np

D_MODEL = 2048
BATCH = 8
SEQ = 2048
DEPTH = 2
DEC_BATCH = 128
DEC_SEQ = 8
PAST_LEN = 2048
PAGE_SIZE = 128

MLSTM_HEADS = 4
MLSTM_HEAD_DIM = 256
MLSTM_WIDTH = MLSTM_HEADS * MLSTM_HEAD_DIM
MLSTM_CHUNK = 64
MOBA_HEADS = 8
MOBA_HEAD_DIM = 128
MOBA_WIDTH = MOBA_HEADS * MOBA_HEAD_DIM
MOBA_BLOCK = 256
MOBA_TOPK = 3
MOBA_QBLOCK = 128
REL_BUCKETS = 32
REL_MAX_DIST = 128
D_FF = 5632
NORM_EPS = 1e-6
PROJ_SPLITS = (MLSTM_WIDTH, MLSTM_WIDTH, MLSTM_WIDTH, MLSTM_WIDTH, MLSTM_HEADS, MLSTM_HEADS,
               MOBA_WIDTH, MOBA_WIDTH, MOBA_WIDTH, D_MODEL, D_MODEL)
PROJ_WIDTH = sum(PROJ_SPLITS)

kernel_name = 'hybrid_mlstm_moba_macaron_step'


def rmsnorm(x, g):
    xf = x.astype(jnp.float32)
    y = xf * lax.rsqrt(jnp.mean(xf * xf, axis=-1, keepdims=True) + NORM_EPS)
    return (y * g.astype(jnp.float32)).astype(x.dtype)


def swiglu(x, wg, wu, wd):
    return (jax.nn.silu(x @ wg) * (x @ wu)) @ wd


def split_proj(p):
    offs = np.cumsum(PROJ_SPLITS)[:-1].tolist()
    return jnp.split(p, offs, axis=-1)


def rel_bucket(dist):
    n = jnp.maximum(dist, 0)
    max_exact = REL_BUCKETS // 2
    nf = jnp.maximum(n, 1).astype(jnp.float32)
    large = max_exact + (jnp.log(nf / max_exact) / math.log(REL_MAX_DIST / max_exact)
                         * (REL_BUCKETS - max_exact)).astype(jnp.int32)
    large = jnp.minimum(large, REL_BUCKETS - 1)
    return jnp.where(n < max_exact, n, large)


def rel_bias(rel_t, dist):
    heads = jnp.arange(rel_t.shape[0])[:, None, None]
    return rel_t[heads, rel_bucket(dist)]


def mlstm_chunk(carry, inp):
    C, n, m = carry
    q, k, v, ig, lf = inp
    L = q.shape[-2]
    q = q.astype(jnp.float32)
    k = k.astype(jnp.float32) * (k.shape[-1] ** -0.5)
    v = v.astype(jnp.float32)
    b = jnp.cumsum(lf, axis=-1)
    causal = jnp.tril(jnp.ones((L, L), dtype=bool))
    log_d = jnp.where(causal, b[..., :, None] - b[..., None, :] + ig[..., None, :], -jnp.inf)
    log_p = b + m[..., None]
    m_t = jnp.maximum(log_p, jnp.max(log_d, axis=-1))
    w_intra = jnp.exp(log_d - m_t[..., None])
    w_prev = jnp.exp(log_p - m_t)
    s = jnp.einsum('bhtd,bhsd->bhts', q, k) * w_intra
    num = jnp.einsum('bhts,bhse->bhte', s, v) + w_prev[..., None] * jnp.einsum('bhed,bhtd->bhte', C, q)
    den = jnp.sum(s, axis=-1) + w_prev * jnp.einsum('bhd,bhtd->bht', n, q)
    h = num / jnp.maximum(jnp.abs(den), jnp.exp(-m_t))[..., None]
    w_last = w_intra[..., -1, :]
    decay = w_prev[..., -1]
    C_new = decay[..., None, None] * C + jnp.einsum('bhs,bhse,bhsd->bhed', w_last, v, k)
    n_new = decay[..., None] * n + jnp.einsum('bhs,bhsd->bhd', w_last, k)
    return (C_new, n_new, m_t[..., -1]), h


def mlstm_run(q, k, v, ig, lf, state):
    bsz, nh, seq, d = q.shape
    L = math.gcd(seq, MLSTM_CHUNK)
    nc = seq // L

    def chunks(a):
        return jnp.moveaxis(a.reshape(bsz, nh, nc, L, *a.shape[3:]), 2, 0)

    state, h = lax.scan(mlstm_chunk, state, tuple(chunks(a) for a in (q, k, v, ig, lf)))
    h = jnp.moveaxis(h, 0, 2).reshape(bsz, nh, seq, d)
    return h, state


def select_blocks(q, kmean, qblk):
    nblk = kmean.shape[-2]
    s = jnp.einsum('...hqd,...hnd->...hqn', q.astype(jnp.float32), kmean.astype(jnp.float32))
    past = jnp.arange(nblk)[None, :] < qblk[:, None]
    s = jnp.where(past, s, -jnp.inf)
    vals, idx = lax.top_k(s, min(MOBA_TOPK, nblk))
    return idx, jnp.isfinite(vals)


def moba_attend(q, qpos, k_own, v_own, pos_own, rel_t, k_sel=None, v_sel=None, pos_sel=None, valid_sel=None):
    qf = q.astype(jnp.float32) * (MOBA_HEAD_DIM ** -0.5)
    d_own = qpos[:, None] - pos_own[None, :]
    lo = jnp.einsum('...hqd,...hnd->...hqn', qf, k_own.astype(jnp.float32))
    lo = jnp.where(d_own >= 0, lo + rel_bias(rel_t, d_own), -jnp.inf)
    if k_sel is None:
        p = jax.nn.softmax(lo, axis=-1)
        out = jnp.einsum('...hqn,...hnd->...hqd', p, v_own.astype(jnp.float32))
        return out.astype(q.dtype)
    d_sel = qpos[:, None] - pos_sel
    ls = jnp.einsum('...hqd,...hqnd->...hqn', qf, k_sel.astype(jnp.float32))
    ls = jnp.where(valid_sel, ls + rel_bias(rel_t, d_sel), -jnp.inf)
    ns = ls.shape[-1]
    p = jax.nn.softmax(jnp.concatenate([ls, lo], axis=-1), axis=-1)
    out = (jnp.einsum('...hqn,...hqnd->...hqd', p[..., :ns], v_sel.astype(jnp.float32))
           + jnp.einsum('...hqn,...hnd->...hqd', p[..., ns:], v_own.astype(jnp.float32)))
    return out.astype(q.dtype)


def moba_prompt(q, k, v, rel_t):
    bsz, nh, seq, dh = q.shape
    nblk = -(-seq // MOBA_BLOCK)
    pad = nblk * MOBA_BLOCK - seq
    kb = jnp.pad(k, ((0, 0), (0, 0), (0, pad), (0, 0))).reshape(bsz, nh, nblk, MOBA_BLOCK, dh)
    vb = jnp.pad(v, ((0, 0), (0, 0), (0, pad), (0, 0))).reshape(bsz, nh, nblk, MOBA_BLOCK, dh)
    kmean = jnp.mean(kb.astype(jnp.float32), axis=3)
    pos = jnp.arange(seq, dtype=jnp.int32)
    idx, valid = select_blocks(q, kmean, pos // MOBA_BLOCK)
    nk = idx.shape[-1]
    nq = seq // MOBA_QBLOCK

    def items(a):
        a = a.reshape(bsz, nh, nq, MOBA_QBLOCK, *a.shape[3:])
        return jnp.moveaxis(a, 2, 1).reshape(bsz * nq, nh, MOBA_QBLOCK, *a.shape[4:])

    heads = jnp.arange(nh)[:, None, None]
    offs = jnp.arange(MOBA_BLOCK, dtype=jnp.int32)

    def one(item):
        bi, qi, q_i, idx_i, valid_i = item
        qpos = qi * MOBA_QBLOCK + jnp.arange(MOBA_QBLOCK, dtype=jnp.int32)
        j = (qi * MOBA_QBLOCK) // MOBA_BLOCK
        kb_i = kb[bi]
        vb_i = vb[bi]
        k_sel = kb_i[heads, idx_i].reshape(nh, MOBA_QBLOCK, nk * MOBA_BLOCK, dh)
        v_sel = vb_i[heads, idx_i].reshape(nh, MOBA_QBLOCK, nk * MOBA_BLOCK, dh)
        pos_sel = (idx_i[..., None] * MOBA_BLOCK + offs).reshape(nh, MOBA_QBLOCK, nk * MOBA_BLOCK)
        valid_sel = jnp.repeat(valid_i, MOBA_BLOCK, axis=-1)
        return moba_attend(q_i, qpos, kb_i[:, j], vb_i[:, j], j * MOBA_BLOCK + offs, rel_t,
                           k_sel, v_sel, pos_sel, valid_sel)

    bidx = jnp.repeat(jnp.arange(bsz, dtype=jnp.int32), nq)
    qidx = jnp.tile(jnp.arange(nq, dtype=jnp.int32), bsz)
    out = lax.map(one, (bidx, qidx, items(q), items(idx), items(valid)))
    return jnp.moveaxis(out.reshape(bsz, nq, nh, MOBA_QBLOCK, dh), 1, 2).reshape(bsz, nh, seq, dh)


def moba_sample(q, k_new, v_new, kpool, vpool, page_table, rel_t):
    bd, nh, t_new, dh = q.shape
    ppb = MOBA_BLOCK // PAGE_SIZE
    n_pages = page_table.shape[1]
    past = n_pages * PAGE_SIZE
    jblk = past // MOBA_BLOCK
    qpos = past + jnp.arange(t_new, dtype=jnp.int32)
    own_pt = page_table[:, jblk * ppb:]
    n_own = (n_pages - jblk * ppb) * PAGE_SIZE
    k_own = jnp.concatenate([jnp.moveaxis(kpool[own_pt], 1, 2).reshape(bd, nh, n_own, dh), k_new], axis=2)
    v_own = jnp.concatenate([jnp.moveaxis(vpool[own_pt], 1, 2).reshape(bd, nh, n_own, dh), v_new], axis=2)
    pos_own = jblk * MOBA_BLOCK + jnp.arange(n_own + t_new, dtype=jnp.int32)
    if jblk == 0:
        return moba_attend(q, qpos, k_own, v_own, pos_own, rel_t)
    kpast = kpool[page_table[:, :jblk * ppb]]
    kmean = kpast.reshape(bd, jblk, ppb, nh, PAGE_SIZE, dh).astype(jnp.float32).mean(axis=(2, 4))
    kmean = jnp.moveaxis(kmean, 2, 1)
    idx, _ = select_blocks(q, kmean, jnp.full((t_new,), jblk, dtype=jnp.int32))
    nk = idx.shape[-1]
    logical = idx[..., None] * ppb + jnp.arange(ppb, dtype=jnp.int32)
    phys = page_table[jnp.arange(bd)[:, None, None, None, None], logical]
    pos_sel = (idx[..., None] * MOBA_BLOCK + jnp.arange(MOBA_BLOCK, dtype=jnp.int32)).reshape(bd, nh, t_new, nk * MOBA_BLOCK)
    heads = jnp.arange(nh)[None, :, None, None]

    def one(item):
        q_t, phys_t, pos_t, qpos_t = item
        k_sel = kpool[phys_t, heads].reshape(bd, nh, 1, nk * MOBA_BLOCK, dh)
        v_sel = vpool[phys_t, heads].reshape(bd, nh, 1, nk * MOBA_BLOCK, dh)
        pos_t = pos_t[:, :, None, :]
        out = moba_attend(q_t[:, :, None, :], qpos_t[None], k_own, v_own, pos_own, rel_t,
                          k_sel, v_sel, pos_t, jnp.ones(pos_t.shape, dtype=bool))
        return out[:, :, 0]

    out = lax.map(one, (jnp.moveaxis(q, 2, 0), jnp.moveaxis(phys, 2, 0), jnp.moveaxis(pos_sel, 2, 0), qpos))
    return jnp.moveaxis(out, 0, 2)


def layer(x, mstate, moba_fn, g_ff1, w_ff1_gate, w_ff1_up, w_ff1_down, g_mix, w_in, b_ig, b_fg,
          g_head, w_a, w_b, w_out, g_ff2, w_ff2_gate, w_ff2_up, w_ff2_down):
    bsz, seq, _ = x.shape
    x = x + 0.5 * swiglu(rmsnorm(x, g_ff1), w_ff1_gate, w_ff1_up, w_ff1_down)
    h = rmsnorm(x, g_mix)
    qm, km, vm, om, ig, fg, qb, kb, vb, ga, gb = split_proj(h @ w_in)

    def heads_of(a, nh, dh):
        return a.reshape(bsz, seq, nh, dh).transpose(0, 2, 1, 3)

    ig = jnp.moveaxis((ig + b_ig).astype(jnp.float32), 2, 1)
    lf = jnp.moveaxis(jax.nn.log_sigmoid((fg + b_fg).astype(jnp.float32)), 2, 1)
    hm, mstate = mlstm_run(heads_of(qm, MLSTM_HEADS, MLSTM_HEAD_DIM), heads_of(km, MLSTM_HEADS, MLSTM_HEAD_DIM),
                           heads_of(vm, MLSTM_HEADS, MLSTM_HEAD_DIM), ig, lf, mstate)
    hm = jax.nn.sigmoid(om).reshape(bsz, seq, MLSTM_HEADS, MLSTM_HEAD_DIM) * jnp.moveaxis(hm, 1, 2).astype(x.dtype)
    hm = rmsnorm(hm, g_head.reshape(MLSTM_HEADS, MLSTM_HEAD_DIM)).reshape(bsz, seq, MLSTM_WIDTH)
    kb = heads_of(kb, MOBA_HEADS, MOBA_HEAD_DIM)
    vb = heads_of(vb, MOBA_HEADS, MOBA_HEAD_DIM)
    hb = moba_fn(heads_of(qb, MOBA_HEADS, MOBA_HEAD_DIM), kb, vb)
    hb = jnp.moveaxis(hb, 1, 2).reshape(bsz, seq, MOBA_WIDTH)
    merged = jax.nn.sigmoid(ga) * (hm @ w_a) + jax.nn.sigmoid(gb) * (hb @ w_b)
    x = x + merged @ w_out
    x = x + 0.5 * swiglu(rmsnorm(x, g_ff2), w_ff2_gate, w_ff2_up, w_ff2_down)
    return x, mstate, kb, vb


def setup_inputs(seed: int = 0) -> dict:
    key = jax.random.key(seed)
    ks = iter(jax.random.split(key, 40))

    def nrm(shape, scale):
        return jax.random.normal(next(ks), shape, jnp.float32) * scale

    n_pages = PAST_LEN // PAGE_SIZE
    n_pool = (DEC_BATCH * n_pages * 5) // 4
    page_table = jax.random.permutation(next(ks), n_pool)[:DEC_BATCH * n_pages].reshape(DEC_BATCH, n_pages).astype(jnp.int32)
    dm, df = D_MODEL ** -0.5, D_FF ** -0.5
    return {
        'x_prompt': nrm((BATCH, SEQ, D_MODEL), 1.0),
        'x_sample': nrm((DEC_BATCH, DEC_SEQ, D_MODEL), 1.0),
        'cache_k': nrm((DEPTH, n_pool, MOBA_HEADS, PAGE_SIZE, MOBA_HEAD_DIM), 1.0),
        'cache_v': nrm((DEPTH, n_pool, MOBA_HEADS, PAGE_SIZE, MOBA_HEAD_DIM), 1.0),
        'state_C': nrm((DEPTH, DEC_BATCH, MLSTM_HEADS, MLSTM_HEAD_DIM, MLSTM_HEAD_DIM), 0.1),
        'state_n': nrm((DEPTH, DEC_BATCH, MLSTM_HEADS, MLSTM_HEAD_DIM), 0.1),
        'state_m': nrm((DEPTH, DEC_BATCH, MLSTM_HEADS), 0.5),
        'page_table': page_table,
        'g_ff1': 1.0 + nrm((DEPTH, D_MODEL), 0.01),
        'w_ff1_gate': nrm((DEPTH, D_MODEL, D_FF), dm),
        'w_ff1_up': nrm((DEPTH, D_MODEL, D_FF), dm),
        'w_ff1_down': nrm((DEPTH, D_FF, D_MODEL), df),
        'g_mix': 1.0 + nrm((DEPTH, D_MODEL), 0.01),
        'w_in': nrm((DEPTH, D_MODEL, PROJ_WIDTH), dm),
        'b_ig': nrm((DEPTH, MLSTM_HEADS), 0.1),
        'b_fg': 3.0 + nrm((DEPTH, MLSTM_HEADS), 0.5),
        'g_head': 1.0 + nrm((DEPTH, MLSTM_WIDTH), 0.01),
        'w_a': nrm((DEPTH, MLSTM_WIDTH, D_MODEL), MLSTM_WIDTH ** -0.5),
        'w_b': nrm((DEPTH, MOBA_WIDTH, D_MODEL), MOBA_WIDTH ** -0.5),
        'w_out': nrm((DEPTH, D_MODEL, D_MODEL), dm),
        'g_ff2': 1.0 + nrm((DEPTH, D_MODEL), 0.01),
        'w_ff2_gate': nrm((DEPTH, D_MODEL, D_FF), dm),
        'w_ff2_up': nrm((DEPTH, D_MODEL, D_FF), dm),
        'w_ff2_down': nrm((DEPTH, D_FF, D_MODEL), df),
        'rel_bias_table': nrm((REL_BUCKETS, MOBA_HEADS), 0.1),
        'g_final': 1.0 + nrm((D_MODEL,), 0.01),
    }


def reference(x_prompt, x_sample, cache_k, cache_v, state_C, state_n, state_m, page_table,
              g_ff1, w_ff1_gate, w_ff1_up, w_ff1_down, g_mix, w_in, b_ig, b_fg, g_head,
              w_a, w_b, w_out, g_ff2, w_ff2_gate, w_ff2_up, w_ff2_down, rel_bias_table, g_final):
    rel_t = rel_bias_table.T.astype(jnp.float32)
    bp = x_prompt.shape[0]
    sdt = state_C.dtype
    xp, xs = x_prompt, x_sample
    kp_l, vp_l, ks_l, vs_l = [], [], [], []
    cp_l, np_l, mp_l, cs_l, ns_l, ms_l = [], [], [], [], [], []
    for l in range(DEPTH):
        lw = (g_ff1[l], w_ff1_gate[l], w_ff1_up[l], w_ff1_down[l], g_mix[l], w_in[l], b_ig[l], b_fg[l],
              g_head[l], w_a[l], w_b[l], w_out[l], g_ff2[l], w_ff2_gate[l], w_ff2_up[l], w_ff2_down[l])
        m0 = (jnp.zeros((bp, MLSTM_HEADS, MLSTM_HEAD_DIM, MLSTM_HEAD_DIM), jnp.float32),
              jnp.zeros((bp, MLSTM_HEADS, MLSTM_HEAD_DIM), jnp.float32),
              jnp.zeros((bp, MLSTM_HEADS), jnp.float32))
        xp, (cp, npp, mp), kp, vp = layer(xp, m0, functools.partial(moba_prompt, rel_t=rel_t), *lw)
        seq = kp.shape[2]
        kp_l.append(jnp.moveaxis(kp.reshape(bp, MOBA_HEADS, seq // PAGE_SIZE, PAGE_SIZE, MOBA_HEAD_DIM), 2, 1))
        vp_l.append(jnp.moveaxis(vp.reshape(bp, MOBA_HEADS, seq // PAGE_SIZE, PAGE_SIZE, MOBA_HEAD_DIM), 2, 1))
        cp_l.append(cp.astype(sdt))
        np_l.append(npp.astype(sdt))
        mp_l.append(mp.astype(sdt))
        ms0 = (state_C[l].astype(jnp.float32), state_n[l].astype(jnp.float32), state_m[l].astype(jnp.float32))
        fn = functools.partial(moba_sample, kpool=cache_k[l], vpool=cache_v[l], page_table=page_table, rel_t=rel_t)
        xs, (cs, ns, ms), ks, vs = layer(xs, ms0, lambda q, k, v, fn=fn: fn(q, k, v), *lw)
        ks_l.append(ks)
        vs_l.append(vs)
        cs_l.append(cs.astype(sdt))
        ns_l.append(ns.astype(sdt))
        ms_l.append(ms.astype(sdt))
    y_prompt = rmsnorm(xp, g_final)
    y_sample = rmsnorm(xs, g_final)
    return (y_prompt, y_sample, jnp.stack(kp_l), jnp.stack(vp_l), jnp.stack(ks_l), jnp.stack(vs_l),
            jnp.stack(cp_l), jnp.stack(np_l), jnp.stack(mp_l), jnp.stack(cs_l), jnp.stack(ns_l), jnp.stack(ms_l))
```

```python
import functools
import math

import numpy as np
import jax
import jax.numpy as jnp
from jax import lax
from jax.experimental import pallas as pl
from jax.experimental.pallas import tpu as pltpu

F32 = jnp.float32
BF16 = jnp.bfloat16

NORM_EPS = 1e-6
MOBA_BLOCK = 256
MOBA_TOPK = 3
MOBA_QBLOCK = 128
REL_BUCKETS = 32
REL_MAX_DIST = 128
MLSTM_PROMPT_CHUNK = 256
LANE = 128
SUBLANE = 8
NEG = -1e30
VMEM_LIMIT = 56 * 1024 * 1024


def _pick(n, cands):
    for c in cands:
        if n % c == 0:
            return c
    return n


def _cparams(sem):
    return pltpu.CompilerParams(dimension_semantics=sem, vmem_limit_bytes=VMEM_LIMIT)


def _rms(x, g):
    return x * lax.rsqrt(jnp.mean(x * x, axis=-1, keepdims=True) + NORM_EPS) * g


def _dot(a, b):
    return jnp.dot(a, b, preferred_element_type=F32)


def _dot_nt(a, b):
    return lax.dot_general(a, b, (((1,), (1,)), ((), ())), preferred_element_type=F32)


def _dot_tn(a, b):
    return lax.dot_general(a, b, (((0,), (0,)), ((), ())), preferred_element_type=F32)


def _ffn_body(x_ref, g_ref, wg_ref, wu_ref, wd_ref, *rest, final):
    if final:
        gf_ref, o_ref, hn_ref = rest
    else:
        o_ref, hn_ref = rest
    f = pl.program_id(1)

    @pl.when(f == 0)
    def _():
        hn_ref[...] = _rms(x_ref[...], g_ref[...]).astype(BF16)
        o_ref[...] = jnp.zeros_like(o_ref)

    h = hn_ref[...]
    g = _dot(h, wg_ref[...])
    u = _dot(h, wu_ref[...])
    a = (g * jax.nn.sigmoid(g) * u).astype(BF16)
    o_ref[...] += _dot(a, wd_ref[...])

    @pl.when(f == pl.num_programs(1) - 1)
    def _():
        y = x_ref[...] + 0.5 * o_ref[...]
        if final:
            y = _rms(y, gf_ref[...])
        o_ref[...] = y


def _ffn(x, g, wg, wu, wd, g_final=None):
    m, d = x.shape
    ff = wg.shape[1]
    tm = _pick(m, (512, 256, 128, 64, 32, 16, 8))
    tf = _pick(ff, (512, 256, 128))
    final = g_final is not None
    in_specs = [
        pl.BlockSpec((tm, d), lambda i, f: (i, 0)),
        pl.BlockSpec((1, d), lambda i, f: (0, 0)),
        pl.BlockSpec((d, tf), lambda i, f: (0, f)),
        pl.BlockSpec((d, tf), lambda i, f: (0, f)),
        pl.BlockSpec((tf, d), lambda i, f: (f, 0)),
    ]
    args = [x, g.reshape(1, d), wg, wu, wd]
    if final:
        in_specs.append(pl.BlockSpec((1, d), lambda i, f: (0, 0)))
        args.append(g_final.reshape(1, d))
    return pl.pallas_call(
        functools.partial(_ffn_body, final=final),
        grid=(m // tm, ff // tf),
        in_specs=in_specs,
        out_specs=pl.BlockSpec((tm, d), lambda i, f: (i, 0)),
        out_shape=jax.ShapeDtypeStruct((m, d), F32),
        scratch_shapes=[pltpu.VMEM((tm, d), BF16)],
        compiler_params=_cparams(("parallel", "arbitrary")),
    )(*args)


def _proj_body(x_ref, g_ref, w_ref, wgate_ref, p_ref, gate_ref, hn_ref):
    @pl.when(pl.program_id(1) == 0)
    def _():
        hn = _rms(x_ref[...], g_ref[...]).astype(BF16)
        hn_ref[...] = hn
        gate_ref[...] = _dot(hn, wgate_ref[...])

    p_ref[...] = _dot(hn_ref[...], w_ref[...])


def _proj(x, g, w_main, w_gate):
    m, d = x.shape
    n = w_main.shape[1]
    tm = _pick(m, (512, 256, 128, 64, 32, 16, 8))
    tn = _pick(n, (512, 256, 128))
    return pl.pallas_call(
        _proj_body,
        grid=(m // tm, n // tn),
        in_specs=[
            pl.BlockSpec((tm, d), lambda i, j: (i, 0)),
            pl.BlockSpec((1, d), lambda i, j: (0, 0)),
            pl.BlockSpec((d, tn), lambda i, j: (0, j)),
            pl.BlockSpec((d, LANE), lambda i, j: (0, 0)),
        ],
        out_specs=[
            pl.BlockSpec((tm, tn), lambda i, j: (i, j)),
            pl.BlockSpec((tm, LANE), lambda i, j: (i, 0)),
        ],
        out_shape=[jax.ShapeDtypeStruct((m, n), F32), jax.ShapeDtypeStruct((m, LANE), F32)],
        scratch_shapes=[pltpu.VMEM((tm, d), BF16)],
        compiler_params=_cparams(("parallel", "arbitrary")),
    )(x, g.reshape(1, d), w_main, w_gate)


def _log_sigmoid(x):
    return jnp.minimum(x, 0.0) - jnp.log(1.0 + jnp.exp(-jnp.abs(x)))


def _mlstm_body(*refs, nh, dh, lq, lk, has_init):
    if has_init:
        (q_ref, k_ref, v_ref, o_ref, gc_ref, gr_ref, bc_ref, br_ref, gh_ref,
         c0_ref, n0_ref, m0_ref, h_ref, cn_ref, nn_ref, mn_ref, c_sc, n_sc, m_sc) = refs
    else:
        (q_ref, k_ref, v_ref, o_ref, gc_ref, gr_ref, bc_ref, br_ref, gh_ref,
         h_ref, cn_ref, nn_ref, mn_ref, c_sc, n_sc, m_sc) = refs
    c = pl.program_id(1)
    last = pl.num_programs(1) - 1

    @pl.when(c == 0)
    def _():
        if has_init:
            c_sc[...] = c0_ref[0]
            n_sc[...] = n0_ref[0]
            m_sc[...] = m0_ref[0]
        else:
            c_sc[...] = jnp.zeros_like(c_sc)
            n_sc[...] = jnp.zeros_like(n_sc)
            m_sc[...] = jnp.zeros_like(m_sc)

    gcol = gc_ref[...] + bc_ref[...]
    grow = gr_ref[0] + br_ref[...]
    lf_col = _log_sigmoid(gcol)
    lf_row = _log_sigmoid(grow)
    if lq >= LANE:
        ri = lax.broadcasted_iota(jnp.int32, (lq, lq), 0)
        ci = lax.broadcasted_iota(jnp.int32, (lq, lq), 1)
        tril = jnp.where(ci <= ri, 1.0, 0.0).astype(F32)
        triu = jnp.where(ri <= ci, 1.0, 0.0).astype(F32)
        b_col = jnp.dot(tril, lf_col, precision=lax.Precision.HIGHEST, preferred_element_type=F32)
        b_row = jnp.dot(lf_row, triu, precision=lax.Precision.HIGHEST, preferred_element_type=F32)
    else:
        ri = lax.broadcasted_iota(jnp.int32, (lq, LANE), 0)
        ci = lax.broadcasted_iota(jnp.int32, (SUBLANE, lk), 1)
        b_col = jnp.zeros((lq, LANE), F32)
        b_row = jnp.zeros((SUBLANE, lk), F32)
        for s in range(lq):
            b_col = b_col + jnp.where(ri >= s, lf_col[s:s + 1, :], 0.0)
            b_row = b_row + jnp.where(ci >= s, lf_row[:, s:s + 1], 0.0)

    qi = lax.broadcasted_iota(jnp.int32, (lq, lk), 0)
    ki = lax.broadcasted_iota(jnp.int32, (lq, lk), 1)
    causal = ki <= qi
    kvalid = lax.broadcasted_iota(jnp.int32, (lk, 1), 0) < lq
    q_all = q_ref[...]
    k_all = k_ref[...]
    v_all = v_ref[...]
    o_all = o_ref[...]
    gh_all = gh_ref[...]
    m_prev_all = m_sc[...]

    def pad_keys(a):
        if lk == lq:
            return a
        return jnp.concatenate([a, jnp.zeros((lk - lq, a.shape[1]), a.dtype)], axis=0)

    for h in range(nh):
        sl = slice(h * dh, (h + 1) * dh)
        qh = q_all[:, sl]
        kh = pad_keys(k_all[:, sl] * (dh ** -0.5))
        vh = pad_keys(v_all[:, sl])
        bq = b_col[:, nh + h:nh + h + 1]
        ig_k = pad_keys(gcol[:, h:h + 1])
        bk = pad_keys(bq)
        r_row = grow[h:h + 1, :] - b_row[nh + h:nh + h + 1, :]
        m_prev = m_prev_all[:, h:h + 1]
        log_d = jnp.where(causal, bq + r_row, -jnp.inf)
        log_p = bq + m_prev
        m_t = jnp.maximum(log_p, jnp.max(log_d, axis=-1, keepdims=True))
        w_intra = jnp.exp(log_d - m_t)
        w_prev = jnp.exp(log_p - m_t)
        qb = qh.astype(BF16)
        kb = kh.astype(BF16)
        s = _dot_nt(qb, kb) * w_intra
        cmat = c_sc[h]
        nrow = n_sc[h:h + 1, :]
        num = _dot(s.astype(BF16), vh.astype(BF16)) + w_prev * _dot_nt(qb, cmat.astype(BF16))
        den = jnp.sum(s, axis=-1, keepdims=True) + w_prev * jnp.sum(qh * nrow, axis=-1, keepdims=True)
        hh = num / jnp.maximum(jnp.abs(den), jnp.exp(-m_t))
        b_last = bq[lq - 1:lq, :]
        m_last = m_t[lq - 1:lq, :]
        w_last = jnp.where(kvalid, jnp.exp(b_last - bk + ig_k - m_last), 0.0)
        decay = w_prev[lq - 1:lq, :]
        c_new = decay * cmat + _dot_tn((vh * w_last).astype(BF16), kb)
        n_new = decay * nrow + jnp.sum(kh * w_last, axis=0, keepdims=True)
        c_sc[h] = c_new
        n_sc[h:h + 1, :] = n_new
        m_sc[:, h:h + 1] = m_last
        hm = jax.nn.sigmoid(o_all[:, sl]) * hh
        h_ref[:, sl] = _rms(hm, gh_all[:, sl])

    @pl.when(c == last)
    def _():
        cn_ref[0] = c_sc[...]
        nn_ref[0] = n_sc[...]
        mn_ref[0] = m_sc[...]


def _mlstm(p, gates, gates_t, bias_c, bias_r, g_head, *, row_off, bsz, seq, chunk, col0, nh, dh, init=None):
    w = nh * dh
    lq = chunk
    lk = max(chunk, LANE)
    nc = seq // chunk
    rb0 = row_off // lq
    has_init = init is not None

    def rowblk(b, c):
        return rb0 + b * nc + c

    in_specs = [
        pl.BlockSpec((lq, w), lambda b, c: (rowblk(b, c), col0)),
        pl.BlockSpec((lq, w), lambda b, c: (rowblk(b, c), col0 + 1)),
        pl.BlockSpec((lq, w), lambda b, c: (rowblk(b, c), col0 + 2)),
        pl.BlockSpec((lq, w), lambda b, c: (rowblk(b, c), col0 + 3)),
        pl.BlockSpec((lq, LANE), lambda b, c: (rowblk(b, c), 0)),
        pl.BlockSpec((1, SUBLANE, lk), lambda b, c: (b, 0, c)),
        pl.BlockSpec((1, LANE), lambda b, c: (0, 0)),
        pl.BlockSpec((SUBLANE, 1), lambda b, c: (0, 0)),
        pl.BlockSpec((1, w), lambda b, c: (0, 0)),
    ]
    args = [p, p, p, p, gates, gates_t, bias_c, bias_r, g_head.reshape(1, w)]
    if has_init:
        c0, n0, m0 = init
        in_specs += [
            pl.BlockSpec((1, nh, dh, dh), lambda b, c: (b, 0, 0, 0)),
            pl.BlockSpec((1, nh, dh), lambda b, c: (b, 0, 0)),
            pl.BlockSpec((1, 1, nh), lambda b, c: (b, 0, 0)),
        ]
        args += [c0, n0, m0.reshape(bsz, 1, nh)]
    out_specs = [
        pl.BlockSpec((lq, w), lambda b, c: (b * nc + c, 0)),
        pl.BlockSpec((1, nh, dh, dh), lambda b, c: (b, 0, 0, 0)),
        pl.BlockSpec((1, nh, dh), lambda b, c: (b, 0, 0)),
        pl.BlockSpec((1, 1, nh), lambda b, c: (b, 0, 0)),
    ]
    out_shape = [
        jax.ShapeDtypeStruct((bsz * seq, w), F32),
        jax.ShapeDtypeStruct((bsz, nh, dh, dh), F32),
        jax.ShapeDtypeStruct((bsz, nh, dh), F32),
        jax.ShapeDtypeStruct((bsz, 1, nh), F32),
    ]
    h, cn, nn, mn = pl.pallas_call(
        functools.partial(_mlstm_body, nh=nh, dh=dh, lq=lq, lk=lk, has_init=has_init),
        grid=(bsz, nc),
        in_specs=in_specs,
        out_specs=out_specs,
        out_shape=out_shape,
        scratch_shapes=[pltpu.VMEM((nh, dh, dh), F32), pltpu.VMEM((nh, dh), F32), pltpu.VMEM((1, nh), F32)],
        compiler_params=_cparams(("parallel", "arbitrary")),
    )(*args)
    return h, (cn, nn, mn.reshape(bsz, nh))


def _bucket_np(dist):
    n = np.maximum(dist, 0)
    max_exact = REL_BUCKETS // 2
    nf = np.maximum(n, 1).astype(np.float32)
    large = max_exact + (np.log(nf / np.float32(max_exact)) / np.float32(math.log(REL_MAX_DIST / max_exact))
                         * np.float32(REL_BUCKETS - max_exact)).astype(np.int32)
    large = np.minimum(large, REL_BUCKETS - 1)
    return np.where(dist < 0, -1, np.where(n < max_exact, n, large)).astype(np.int32)


def _bias_body(tbl_ref, map_ref, o_ref, *, ntile):
    h = pl.program_id(0)
    for t in range(ntile):
        bm = map_ref[t]
        acc = jnp.full(bm.shape, NEG, F32)
        for b in range(REL_BUCKETS):
            acc = jnp.where(bm == b, tbl_ref[h, b], acc)
        o_ref[0, t] = acc


def _bias_tiles(rel_t, maps):
    nhead = rel_t.shape[0]
    ntile, r, c = maps.shape
    return pl.pallas_call(
        functools.partial(_bias_body, ntile=ntile),
        grid=(nhead,),
        in_specs=[
            pl.BlockSpec(memory_space=pltpu.SMEM),
            pl.BlockSpec((ntile, r, c), lambda h: (0, 0, 0)),
        ],
        out_specs=pl.BlockSpec((1, ntile, r, c), lambda h: (h, 0, 0, 0)),
        out_shape=jax.ShapeDtypeStruct((nhead, ntile, r, c), F32),
        compiler_params=_cparams(("arbitrary",)),
    )(rel_t, jnp.asarray(maps))


def _prompt_bias_maps():
    r = np.arange(MOBA_QBLOCK)[:, None]
    c = np.arange(MOBA_BLOCK)[None, :]
    qpb = MOBA_BLOCK // MOBA_QBLOCK
    tiles = [_bucket_np(par * MOBA_QBLOCK + r - c) for par in range(qpb)]
    tiles += [_bucket_np(MOBA_BLOCK + par * MOBA_QBLOCK + r - c) for par in range(qpb)]
    tiles += [_bucket_np(2 * MOBA_BLOCK + r - c)]
    return np.stack(tiles)


def _sample_bias_maps(t_new):
    r = np.arange(t_new)[:, None]
    c = np.arange(MOBA_BLOCK)[None, :]
    own = np.where(c < t_new, r - c, -1)
    return np.stack([_bucket_np(MOBA_BLOCK + r - c), _bucket_np(own), _bucket_np(2 * MOBA_BLOCK + r - c + t_new)])


def _topk_select(scores, valid):
    n = len(scores)
    sel = []
    for a in range(n):
        cnt = jnp.zeros(scores[a].shape, F32)
        for b in range(n):
            if b == a:
                continue
            ahead = (scores[b] >= scores[a]) if b < a else (scores[b] > scores[a])
            cnt = cnt + jnp.where(ahead, 1.0, 0.0)
        keep = cnt < MOBA_TOPK
        sel.append(keep)
    return sel


def _moba_p_body(q_ref, k_ref, v_ref, bias_ref, o_ref, kr_sc, sel_sc, *, nblk, dh):
    i = pl.program_id(2)
    qpb = MOBA_BLOCK // MOBA_QBLOCK
    j = i // qpb
    par = i % qpb

    @pl.when(i == 0)
    def _():
        for b in range(nblk):
            km = jnp.mean(k_ref[b * MOBA_BLOCK:(b + 1) * MOBA_BLOCK, :], axis=0, keepdims=True)
            kr_sc[b] = jnp.broadcast_to(km, (MOBA_QBLOCK, dh)).astype(BF16)

    q = q_ref[...]
    qraw = q.astype(BF16)
    scores = [jnp.where(b < j, _dot_nt(qraw, kr_sc[b]), -jnp.inf) for b in range(nblk)]
    sel = _topk_select(scores, None)
    for b in range(nblk):
        sel_sc[b] = jnp.where(jnp.logical_and(sel[b], b < j), 1.0, 0.0)

    qs = (q * (dh ** -0.5)).astype(BF16)

    def tile(b):
        off = pl.multiple_of(b * MOBA_BLOCK, MOBA_BLOCK)
        kb = k_ref[pl.ds(off, MOBA_BLOCK), :].astype(BF16)
        vb = v_ref[pl.ds(off, MOBA_BLOCK), :].astype(BF16)
        return _dot_nt(qs, kb), vb

    lg, vb = tile(j)
    lg = lg + bias_ref[0, par]
    m = jnp.max(lg, axis=-1, keepdims=True)
    p = jnp.exp(lg - m)
    l = jnp.sum(p, axis=-1, keepdims=True)
    acc = _dot(p.astype(BF16), vb)

    def step(b, carry):
        m, l, acc = carry
        lg, vb = tile(b)
        kind = jnp.where(b == j - 1, qpb + par, 2 * qpb)
        lg = lg + bias_ref[0, kind]
        s1 = sel_sc[b]
        keep = jnp.concatenate([s1] * (MOBA_BLOCK // MOBA_QBLOCK), axis=1) > 0.5
        lg = jnp.where(keep, lg, NEG)
        m_new = jnp.maximum(m, jnp.max(lg, axis=-1, keepdims=True))
        alpha = jnp.exp(m - m_new)
        p = jnp.exp(lg - m_new)
        l = alpha * l + jnp.sum(p, axis=-1, keepdims=True)
        acc = alpha * acc + _dot(p.astype(BF16), vb)
        return m_new, l, acc

    m, l, acc = lax.fori_loop(0, j, step, (m, l, acc))
    o_ref[...] = acc / l


def _moba_prompt(p, bias, *, bsz, seq, colq, colk, colv, nhead, dh):
    assert dh == LANE and MOBA_QBLOCK == LANE and seq % MOBA_BLOCK == 0
    nq = seq // MOBA_QBLOCK
    nblk = seq // MOBA_BLOCK
    ntile = bias.shape[1]
    return pl.pallas_call(
        functools.partial(_moba_p_body, nblk=nblk, dh=dh),
        grid=(bsz, nhead, nq),
        in_specs=[
            pl.BlockSpec((MOBA_QBLOCK, dh), lambda b, h, i: (b * nq + i, colq + h)),
            pl.BlockSpec((seq, dh), lambda b, h, i: (b, colk + h)),
            pl.BlockSpec((seq, dh), lambda b, h, i: (b, colv + h)),
            pl.BlockSpec((1, ntile, MOBA_QBLOCK, MOBA_BLOCK), lambda b, h, i: (h, 0, 0, 0)),
        ],
        out_specs=pl.BlockSpec((MOBA_QBLOCK, dh), lambda b, h, i: (b * nq + i, h)),
        out_shape=jax.ShapeDtypeStruct((bsz * seq, nhead * dh), F32),
        scratch_shapes=[pltpu.VMEM((nblk, MOBA_QBLOCK, dh), BF16), pltpu.VMEM((nblk, MOBA_QBLOCK, dh), F32)],
        compiler_params=_cparams(("parallel", "parallel", "arbitrary")),
    )(p, p, p, bias)


def _moba_s_body(pt_ref, q_ref, kn_ref, vn_ref, bias_ref, *rest, npage, nhead, dh, t_new, page):
    kp = rest[:npage]
    vp = rest[npage:2 * npage]
    o_ref, p_sc, l_sc, acc_sc = rest[2 * npage:]
    ph = pl.program_id(1)
    ppb = MOBA_BLOCK // page
    nblk = npage // ppb

    def heads(x):
        return jnp.stack([x[:, h * dh:(h + 1) * dh] for h in range(nhead)], axis=0)

    def pad_rows(x):
        return jnp.concatenate([x, jnp.zeros((page - t_new, x.shape[1]), x.dtype)], axis=0)

    @pl.when(ph == 0)
    def _():
        q3 = (heads(q_ref[...]) * (dh ** -0.5)).astype(BF16)
        lgs = [jnp.einsum('htd,hkd->htk', q3, kp[pg][0].astype(BF16), preferred_element_type=F32)
               for pg in range(npage)]
        scores = []
        for b in range(nblk):
            tot = lgs[b * ppb]
            for r in range(1, ppb):
                tot = tot + lgs[b * ppb + r]
            scores.append(jnp.sum(tot, axis=-1, keepdims=True))
        sel = _topk_select(scores, None)
        far = bias_ref[:, 2]
        prev = bias_ref[:, 0]
        ml = []
        for pg in range(npage):
            b = pg // ppb
            r = pg % ppb
            bias = prev[:, :, r * page:(r + 1) * page] if b == nblk - 1 else far[:, :, :page]
            ml.append(jnp.where(sel[b], lgs[pg] + bias, NEG))
        kn3 = heads(pad_rows(kn_ref[...])).astype(BF16)
        vn3 = heads(pad_rows(vn_ref[...])).astype(BF16)
        lo = jnp.einsum('htd,hkd->htk', q3, kn3, preferred_element_type=F32) + bias_ref[:, 1][:, :, :page]
        m = jnp.max(lo, axis=-1, keepdims=True)
        for x in ml:
            m = jnp.maximum(m, jnp.max(x, axis=-1, keepdims=True))
        eo = jnp.exp(lo - m)
        l = jnp.sum(eo, axis=-1, keepdims=True)
        for pg in range(npage):
            e = jnp.exp(ml[pg] - m)
            l = l + jnp.sum(e, axis=-1, keepdims=True)
            p_sc[pg] = e
        l_sc[...] = l
        acc_sc[...] = jnp.einsum('htk,hkd->htd', eo.astype(BF16), vn3, preferred_element_type=F32)

    @pl.when(ph == 1)
    def _():
        acc = acc_sc[...]
        for pg in range(npage):
            acc = acc + jnp.einsum('htk,hkd->htd', p_sc[pg].astype(BF16), vp[pg][0].astype(BF16),
                                   preferred_element_type=F32)
        out = acc / l_sc[...]
        for h in range(nhead):
            o_ref[:, h * dh:(h + 1) * dh] = out[h]


def _moba_sample(p, bias, kpool, vpool, page_table, *, row_off, bsz, t_new, colq, colk, colv, nhead, dh):
    npage = page_table.shape[1]
    page = kpool.shape[2]
    assert dh == LANE and page == LANE and MOBA_BLOCK % page == 0
    assert (npage * page) % MOBA_BLOCK == 0 and npage * page >= MOBA_BLOCK and t_new <= page
    w = nhead * dh
    rb0 = row_off // t_new
    ntile = bias.shape[1]
    pt = page_table.reshape(-1).astype(jnp.int32)
    assert colq % nhead == 0 and colk % nhead == 0 and colv % nhead == 0
    colq, colk, colv = colq // nhead, colk // nhead, colv // nhead

    def kmap(pg):
        return lambda b, ph, pt_ref: (pt_ref[b * npage + pg], 0, 0, 0)

    def vmap_(pg):
        return lambda b, ph, pt_ref: (pt_ref[jnp.maximum(b + ph - 1, 0) * npage + pg], 0, 0, 0)

    in_specs = [
        pl.BlockSpec((t_new, w), lambda b, ph, pt_ref: (rb0 + b, colq)),
        pl.BlockSpec((t_new, w), lambda b, ph, pt_ref: (rb0 + b, colk)),
        pl.BlockSpec((t_new, w), lambda b, ph, pt_ref: (rb0 + b, colv)),
        pl.BlockSpec((nhead, ntile, t_new, MOBA_BLOCK), lambda b, ph, pt_ref: (0, 0, 0, 0)),
    ]
    in_specs += [pl.BlockSpec((1, nhead, page, dh), kmap(pg)) for pg in range(npage)]
    in_specs += [pl.BlockSpec((1, nhead, page, dh), vmap_(pg)) for pg in range(npage)]
    grid_spec = pltpu.PrefetchScalarGridSpec(
        num_scalar_prefetch=1,
        grid=(bsz, 2),
        in_specs=in_specs,
        out_specs=pl.BlockSpec((t_new, w), lambda b, ph, pt_ref: (b, 0)),
        scratch_shapes=[
            pltpu.VMEM((npage, nhead, t_new, page), F32),
            pltpu.VMEM((nhead, t_new, 1), F32),
            pltpu.VMEM((nhead, t_new, dh), F32),
        ],
    )
    return pl.pallas_call(
        functools.partial(_moba_s_body, npage=npage, nhead=nhead, dh=dh, t_new=t_new, page=page),
        grid_spec=grid_spec,
        out_shape=jax.ShapeDtypeStruct((bsz * t_new, w), F32),
        compiler_params=_cparams(("arbitrary", "arbitrary")),
    )(pt, p, p, p, bias, *([kpool] * npage), *([vpool] * npage))


def _mix_body(hm_ref, hb_ref, ga_ref, gb_ref, x_ref, wa_ref, wb_ref, wo_ref, o_ref):
    a = _dot(hm_ref[...].astype(BF16), wa_ref[...])
    b = _dot(hb_ref[...].astype(BF16), wb_ref[...])
    merged = jax.nn.sigmoid(ga_ref[...]) * a + jax.nn.sigmoid(gb_ref[...]) * b
    o_ref[...] = x_ref[...] + _dot(merged.astype(BF16), wo_ref[...])


def _mix(hm, hb, p, x, wa, wb, wo):
    m, d = x.shape
    wm = hm.shape[1]
    wbw = hb.shape[1]
    tm = _pick(m, (256, 128, 64, 32, 16, 8))
    const = dict(pipeline_mode=pl.Buffered(1))
    return pl.pallas_call(
        _mix_body,
        grid=(m // tm,),
        in_specs=[
            pl.BlockSpec((tm, wm), lambda i: (i, 0)),
            pl.BlockSpec((tm, wbw), lambda i: (i, 0)),
            pl.BlockSpec((tm, d), lambda i: (i, 0)),
            pl.BlockSpec((tm, d), lambda i: (i, 1)),
            pl.BlockSpec((tm, d), lambda i: (i, 0)),
            pl.BlockSpec((wm, d), lambda i: (0, 0), **const),
            pl.BlockSpec((wbw, d), lambda i: (0, 0), **const),
            pl.BlockSpec((d, d), lambda i: (0, 0), **const),
        ],
        out_specs=pl.BlockSpec((tm, d), lambda i: (i, 0)),
        out_shape=jax.ShapeDtypeStruct((m, d), F32),
        compiler_params=_cparams(("parallel",)),
    )(hm, hb, p, p, x, wa, wb, wo)


def kernel(x_prompt, x_sample, cache_k, cache_v, state_C, state_n, state_m, page_table, g_ff1, w_ff1_gate,
           w_ff1_up, w_ff1_down, g_mix, w_in, b_ig, b_fg, g_head, w_a, w_b, w_out, g_ff2, w_ff2_gate,
           w_ff2_up, w_ff2_down, rel_bias_table, g_final):
    bp, seq, d = x_prompt.shape
    bd, t_new, _ = x_sample.shape
    depth = g_ff1.shape[0]
    nh_m = b_ig.shape[1]
    w_m = w_a.shape[1]
    dh_m = w_m // nh_m
    nh_b = rel_bias_table.shape[1]
    w_bw = w_b.shape[1]
    dh_b = w_bw // nh_b
    page = cache_k.shape[3]
    mp = bp * seq
    sdt = state_C.dtype
    assert 2 * nh_m <= SUBLANE and (2 * d) % w_m == 0 and (2 * d + 4 * w_m) % LANE == 0
    assert mp % t_new == 0 and seq % MLSTM_PROMPT_CHUNK == 0 and seq % page == 0

    o_gate = 4 * w_m
    o_moba = o_gate + 2 * nh_m
    o_ga = o_moba + 3 * w_bw
    col_m = (2 * d) // w_m
    col_b = (2 * d + 4 * w_m) // dh_b
    colq_b, colk_b, colv_b = col_b, col_b + nh_b, col_b + 2 * nh_b

    rel_t = rel_bias_table.T.astype(F32)
    bias_p = _bias_tiles(rel_t, _prompt_bias_maps())
    bias_s = _bias_tiles(rel_t, _sample_bias_maps(t_new))

    x = jnp.concatenate([x_prompt.reshape(mp, d), x_sample.reshape(bd * t_new, d)], axis=0)
    kp_l, vp_l, ks_l, vs_l = [], [], [], []
    cp_l, np_l, mp_l, cs_l, ns_l, ms_l = [], [], [], [], [], []
    for l in range(depth):
        wi = w_in[l]
        w_main = jnp.concatenate([wi[:, o_ga:], wi[:, :o_gate], wi[:, o_moba:o_ga]], axis=1).astype(BF16)
        w_gate = jnp.pad(wi[:, o_gate:o_moba], ((0, 0), (0, LANE - 2 * nh_m))).astype(BF16)
        gate_bias = jnp.concatenate([b_ig[l], b_fg[l]]).astype(F32)
        bias_c = jnp.pad(gate_bias, (0, LANE - 2 * nh_m)).reshape(1, LANE)
        bias_r = jnp.pad(gate_bias, (0, SUBLANE - 2 * nh_m)).reshape(SUBLANE, 1)

        x = _ffn(x, g_ff1[l], w_ff1_gate[l].astype(BF16), w_ff1_up[l].astype(BF16), w_ff1_down[l].astype(BF16))
        p, gates = _proj(x, g_mix[l], w_main, w_gate)

        gt_p = gates[:mp, :SUBLANE].reshape(bp, seq, SUBLANE).transpose(0, 2, 1)
        gt_s = gates[mp:, :SUBLANE].reshape(bd, t_new, SUBLANE).transpose(0, 2, 1)
        gt_s = jnp.pad(gt_s, ((0, 0), (0, 0), (0, LANE - t_new)))

        mk = dict(col0=col_m, nh=nh_m, dh=dh_m)
        hm_p, (cp, npp, mpp) = _mlstm(p, gates, gt_p, bias_c, bias_r, g_head[l], row_off=0, bsz=bp, seq=seq,
                                      chunk=MLSTM_PROMPT_CHUNK, **mk)
        init = (state_C[l].astype(F32), state_n[l].astype(F32), state_m[l].astype(F32))
        hm_s, (cs, ns, ms) = _mlstm(p, gates, gt_s, bias_c, bias_r, g_head[l], row_off=mp, bsz=bd, seq=t_new,
                                    chunk=t_new, init=init, **mk)

        bk = dict(colq=colq_b, colk=colk_b, colv=colv_b, nhead=nh_b, dh=dh_b)
        hb_p = _moba_prompt(p, bias_p, bsz=bp, seq=seq, **bk)
        hb_s = _moba_sample(p, bias_s, cache_k[l], cache_v[l], page_table, row_off=mp, bsz=bd, t_new=t_new, **bk)

        hm = jnp.concatenate([hm_p, hm_s], axis=0)
        hb = jnp.concatenate([hb_p, hb_s], axis=0)
        x = _mix(hm, hb, p, x, w_a[l].astype(BF16), w_b[l].astype(BF16), w_out[l].astype(BF16))
        x = _ffn(x, g_ff2[l], w_ff2_gate[l].astype(BF16), w_ff2_up[l].astype(BF16), w_ff2_down[l].astype(BF16),
                 g_final=g_final if l == depth - 1 else None)

        kcols = p[:, colk_b * dh_b:(colk_b + nh_b) * dh_b]
        vcols = p[:, colv_b * dh_b:(colv_b + nh_b) * dh_b]

        def pages(a):
            return a[:mp].reshape(bp, seq // page, page, nh_b, dh_b).transpose(0, 1, 3, 2, 4)

        def rows(a):
            return a[mp:].reshape(bd, t_new, nh_b, dh_b).transpose(0, 2, 1, 3)

        kp_l.append(pages(kcols))
        vp_l.append(pages(vcols))
        ks_l.append(rows(kcols))
        vs_l.append(rows(vcols))
        cp_l.append(cp.astype(sdt))
        np_l.append(npp.astype(sdt))
        mp_l.append(mpp.astype(sdt))
        cs_l.append(cs.astype(sdt))
        ns_l.append(ns.astype(sdt))
        ms_l.append(ms.astype(sdt))

    y_prompt = x[:mp].reshape(bp, seq, d)
    y_sample = x[mp:].reshape(bd, t_new, d)
    return (y_prompt, y_sample, jnp.stack(kp_l), jnp.stack(vp_l), jnp.stack(ks_l), jnp.stack(vs_l),
            jnp.stack(cp_l), jnp.stack(np_l), jnp.stack(mp_l), jnp.stack(cs_l), jnp.stack(ns_l), jnp.stack(ms_l))
```

```python
import functools
import math

import numpy as np
import jax
import jax.numpy as jnp
from jax import lax
from jax.experimental import pallas as pl
from jax.experimental.pallas import tpu as pltpu

F32 = jnp.float32
BF16 = jnp.bfloat16

NORM_EPS = 1e-6
MOBA_BLOCK = 256
MOBA_TOPK = 3
MOBA_QBLOCK = 128
REL_BUCKETS = 32
REL_MAX_DIST = 128
MLSTM_PROMPT_CHUNK = 256
LANE = 128
SUBLANE = 8
NEG = -1e30
VMEM_LIMIT = 56 * 1024 * 1024


def _pick(n, cands):
    for c in cands:
        if n % c == 0:
            return c
    return n


def _cparams(sem):
    return pltpu.CompilerParams(dimension_semantics=sem, vmem_limit_bytes=VMEM_LIMIT)


def _rms(x, g):
    return x * lax.rsqrt(jnp.mean(x * x, axis=-1, keepdims=True) + NORM_EPS) * g


def _dot(a, b):
    return jnp.dot(a, b, preferred_element_type=F32)


def _dot_nt(a, b):
    return lax.dot_general(a, b, (((1,), (1,)), ((), ())), preferred_element_type=F32)


def _dot_tn(a, b):
    return lax.dot_general(a, b, (((0,), (0,)), ((), ())), preferred_element_type=F32)


def _ffn_body(x_ref, g_ref, wg_ref, wu_ref, wd_ref, *rest, final):
    if final:
        gf_ref, o_ref, hn_ref = rest
    else:
        o_ref, hn_ref = rest
    f = pl.program_id(1)

    @pl.when(f == 0)
    def _():
        hn_ref[...] = _rms(x_ref[...], g_ref[...]).astype(BF16)
        o_ref[...] = jnp.zeros_like(o_ref)

    h = hn_ref[...]
    g = _dot(h, wg_ref[...])
    u = _dot(h, wu_ref[...])
    a = (g * jax.nn.sigmoid(g) * u).astype(BF16)
    o_ref[...] += _dot(a, wd_ref[...])

    @pl.when(f == pl.num_programs(1) - 1)
    def _():
        y = x_ref[...] + 0.5 * o_ref[...]
        if final:
            y = _rms(y, gf_ref[...])
        o_ref[...] = y


def _ffn(x, g, wg, wu, wd, g_final=None):
    m, d = x.shape
    ff = wg.shape[1]
    tm = _pick(m, (512, 256, 128, 64, 32, 16, 8))
    tf = _pick(ff, (512, 256, 128))
    final = g_final is not None
    in_specs = [
        pl.BlockSpec((tm, d), lambda i, f: (i, 0)),
        pl.BlockSpec((1, d), lambda i, f: (0, 0)),
        pl.BlockSpec((d, tf), lambda i, f: (0, f)),
        pl.BlockSpec((d, tf), lambda i, f: (0, f)),
        pl.BlockSpec((tf, d), lambda i, f: (f, 0)),
    ]
    args = [x, g.reshape(1, d), wg, wu, wd]
    if final:
        in_specs.append(pl.BlockSpec((1, d), lambda i, f: (0, 0)))
        args.append(g_final.reshape(1, d))
    return pl.pallas_call(
        functools.partial(_ffn_body, final=final),
        grid=(m // tm, ff // tf),
        in_specs=in_specs,
        out_specs=pl.BlockSpec((tm, d), lambda i, f: (i, 0)),
        out_shape=jax.ShapeDtypeStruct((m, d), F32),
        scratch_shapes=[pltpu.VMEM((tm, d), BF16)],
        compiler_params=_cparams(("parallel", "arbitrary")),
    )(*args)


def _proj_body(x_ref, g_ref, w_ref, wgate_ref, p_ref, gate_ref, hn_ref):
    @pl.when(pl.program_id(1) == 0)
    def _():
        hn = _rms(x_ref[...], g_ref[...]).astype(BF16)
        hn_ref[...] = hn
        gate_ref[...] = _dot(hn, wgate_ref[...])

    p_ref[...] = _dot(hn_ref[...], w_ref[...])


def _proj(x, g, w_main, w_gate):
    m, d = x.shape
    n = w_main.shape[1]
    tm = _pick(m, (1024, 512, 256, 128, 64, 32, 16, 8))
    tn = _pick(n, (1024, 512, 256, 128))
    return pl.pallas_call(
        _proj_body,
        grid=(m // tm, n // tn),
        in_specs=[
            pl.BlockSpec((tm, d), lambda i, j: (i, 0), pipeline_mode=pl.Buffered(1)),
            pl.BlockSpec((1, d), lambda i, j: (0, 0)),
            pl.BlockSpec((d, tn), lambda i, j: (0, j)),
            pl.BlockSpec((d, LANE), lambda i, j: (0, 0)),
        ],
        out_specs=[
            pl.BlockSpec((tm, tn), lambda i, j: (i, j)),
            pl.BlockSpec((tm, LANE), lambda i, j: (i, 0)),
        ],
        out_shape=[jax.ShapeDtypeStruct((m, n), F32), jax.ShapeDtypeStruct((m, LANE), F32)],
        scratch_shapes=[pltpu.VMEM((tm, d), BF16)],
        compiler_params=_cparams(("parallel", "arbitrary")),
    )(x, g.reshape(1, d), w_main, w_gate)


def _log_sigmoid(x):
    return jnp.minimum(x, 0.0) - jnp.log(1.0 + jnp.exp(-jnp.abs(x)))


def _mlstm_body(*refs, nh, dh, lq, lk, has_init):
    if has_init:
        (q_ref, k_ref, v_ref, o_ref, gc_ref, gr_ref, bc_ref, br_ref, gh_ref,
         c0_ref, n0_ref, m0_ref, h_ref, cn_ref, nn_ref, mn_ref, c_sc, n_sc, m_sc) = refs
    else:
        (q_ref, k_ref, v_ref, o_ref, gc_ref, gr_ref, bc_ref, br_ref, gh_ref,
         h_ref, cn_ref, nn_ref, mn_ref, c_sc, n_sc, m_sc) = refs
    c = pl.program_id(1)
    last = pl.num_programs(1) - 1

    @pl.when(c == 0)
    def _():
        if has_init:
            c_sc[...] = c0_ref[0, 0].astype(F32)
            n_sc[...] = n0_ref[0, 0].astype(F32)
            m_sc[...] = m0_ref[0, 0].astype(F32)
        else:
            c_sc[...] = jnp.zeros_like(c_sc)
            n_sc[...] = jnp.zeros_like(n_sc)
            m_sc[...] = jnp.zeros_like(m_sc)

    gcol = gc_ref[...] + bc_ref[...]
    grow = gr_ref[0] + br_ref[...]
    lf_col = _log_sigmoid(gcol)
    lf_row = _log_sigmoid(grow)
    if lq >= LANE:
        ri = lax.broadcasted_iota(jnp.int32, (lq, lq), 0)
        ci = lax.broadcasted_iota(jnp.int32, (lq, lq), 1)
        tril = jnp.where(ci <= ri, 1.0, 0.0).astype(F32)
        triu = jnp.where(ri <= ci, 1.0, 0.0).astype(F32)
        b_col = jnp.dot(tril, lf_col, precision=lax.Precision.HIGHEST, preferred_element_type=F32)
        b_row = jnp.dot(lf_row, triu, precision=lax.Precision.HIGHEST, preferred_element_type=F32)
    else:
        ri = lax.broadcasted_iota(jnp.int32, (lq, LANE), 0)
        ci = lax.broadcasted_iota(jnp.int32, (SUBLANE, lk), 1)
        b_col = jnp.zeros((lq, LANE), F32)
        b_row = jnp.zeros((SUBLANE, lk), F32)
        for s in range(lq):
            b_col = b_col + jnp.where(ri >= s, lf_col[s:s + 1, :], 0.0)
            b_row = b_row + jnp.where(ci >= s, lf_row[:, s:s + 1], 0.0)

    qi = lax.broadcasted_iota(jnp.int32, (lq, lk), 0)
    ki = lax.broadcasted_iota(jnp.int32, (lq, lk), 1)
    causal = ki <= qi
    kvalid = lax.broadcasted_iota(jnp.int32, (lk, 1), 0) < lq
    q_all = q_ref[...]
    k_all = k_ref[...]
    v_all = v_ref[...]
    o_all = o_ref[...]
    gh_all = gh_ref[...]
    m_prev_all = m_sc[...]

    def pad_keys(a):
        if lk == lq:
            return a
        return jnp.concatenate([a, jnp.zeros((lk - lq, a.shape[1]), a.dtype)], axis=0)

    for h in range(nh):
        sl = slice(h * dh, (h + 1) * dh)
        qh = q_all[:, sl]
        kh = pad_keys(k_all[:, sl] * (dh ** -0.5))
        vh = pad_keys(v_all[:, sl])
        bq = b_col[:, nh + h:nh + h + 1]
        ig_k = pad_keys(gcol[:, h:h + 1])
        bk = pad_keys(bq)
        r_row = grow[h:h + 1, :] - b_row[nh + h:nh + h + 1, :]
        m_prev = m_prev_all[:, h:h + 1]
        log_d = jnp.where(causal, bq + r_row, -jnp.inf)
        log_p = bq + m_prev
        m_t = jnp.maximum(log_p, jnp.max(log_d, axis=-1, keepdims=True))
        w_intra = jnp.exp(log_d - m_t)
        w_prev = jnp.exp(log_p - m_t)
        qb = qh.astype(BF16)
        kb = kh.astype(BF16)
        s = _dot_nt(qb, kb) * w_intra
        cmat = c_sc[h]
        nrow = n_sc[h:h + 1, :]
        num = _dot(s.astype(BF16), vh.astype(BF16)) + w_prev * _dot_nt(qb, cmat.astype(BF16))
        den = jnp.sum(s, axis=-1, keepdims=True) + w_prev * jnp.sum(qh * nrow, axis=-1, keepdims=True)
        hh = num / jnp.maximum(jnp.abs(den), jnp.exp(-m_t))
        b_last = bq[lq - 1:lq, :]
        m_last = m_t[lq - 1:lq, :]
        w_last = jnp.where(kvalid, jnp.exp(b_last - bk + ig_k - m_last), 0.0)
        decay = w_prev[lq - 1:lq, :]
        c_new = decay * cmat + _dot_tn((vh * w_last).astype(BF16), kb)
        n_new = decay * nrow + jnp.sum(kh * w_last, axis=0, keepdims=True)
        c_sc[h] = c_new
        n_sc[h:h + 1, :] = n_new
        m_sc[:, h:h + 1] = m_last
        hm = jax.nn.sigmoid(o_all[:, sl]) * hh
        h_ref[:, sl] = _rms(hm, gh_all[:, sl])

    @pl.when(c == last)
    def _():
        cn_ref[0] = c_sc[...]
        nn_ref[0] = n_sc[...]
        mn_ref[0] = m_sc[...]


def _mlstm(p, gates, gates_t, bias_c, bias_r, g_head, *, row_off, bsz, seq, chunk, col0, nh, dh, init=None):
    w = nh * dh
    lq = chunk
    lk = max(chunk, LANE)
    nc = seq // chunk
    rb0 = row_off // lq
    has_init = init is not None

    def rowblk(b, c):
        return rb0 + b * nc + c

    in_specs = [
        pl.BlockSpec((lq, w), lambda b, c: (rowblk(b, c), col0)),
        pl.BlockSpec((lq, w), lambda b, c: (rowblk(b, c), col0 + 1)),
        pl.BlockSpec((lq, w), lambda b, c: (rowblk(b, c), col0 + 2)),
        pl.BlockSpec((lq, w), lambda b, c: (rowblk(b, c), col0 + 3)),
        pl.BlockSpec((lq, LANE), lambda b, c: (rowblk(b, c), 0)),
        pl.BlockSpec((1, SUBLANE, lk), lambda b, c: (b, 0, c)),
        pl.BlockSpec((1, LANE), lambda b, c: (0, 0)),
        pl.BlockSpec((SUBLANE, 1), lambda b, c: (0, 0)),
        pl.BlockSpec((1, w), lambda b, c: (0, 0)),
    ]
    args = [p, p, p, p, gates, gates_t, bias_c, bias_r, g_head.reshape(1, w)]
    if has_init:
        c0, n0, m0, layer = init
        in_specs += [
            pl.BlockSpec((1, 1, nh, dh, dh), lambda b, c: (layer, b, 0, 0, 0)),
            pl.BlockSpec((1, 1, nh, dh), lambda b, c: (layer, b, 0, 0)),
            pl.BlockSpec((1, 1, 1, nh), lambda b, c: (layer, b, 0, 0)),
        ]
        args += [c0, n0, m0.reshape(m0.shape[0], bsz, 1, nh)]
    out_specs = [
        pl.BlockSpec((lq, w), lambda b, c: (b * nc + c, 0)),
        pl.BlockSpec((1, nh, dh, dh), lambda b, c: (b, 0, 0, 0)),
        pl.BlockSpec((1, nh, dh), lambda b, c: (b, 0, 0)),
        pl.BlockSpec((1, 1, nh), lambda b, c: (b, 0, 0)),
    ]
    out_shape = [
        jax.ShapeDtypeStruct((bsz * seq, w), F32),
        jax.ShapeDtypeStruct((bsz, nh, dh, dh), F32),
        jax.ShapeDtypeStruct((bsz, nh, dh), F32),
        jax.ShapeDtypeStruct((bsz, 1, nh), F32),
    ]
    h, cn, nn, mn = pl.pallas_call(
        functools.partial(_mlstm_body, nh=nh, dh=dh, lq=lq, lk=lk, has_init=has_init),
        grid=(bsz, nc),
        in_specs=in_specs,
        out_specs=out_specs,
        out_shape=out_shape,
        scratch_shapes=[pltpu.VMEM((nh, dh, dh), F32), pltpu.VMEM((nh, dh), F32), pltpu.VMEM((1, nh), F32)],
        compiler_params=_cparams(("parallel", "arbitrary")),
    )(*args)
    return h, (cn, nn, mn.reshape(bsz, nh))


def _bucket_np(dist):
    n = np.maximum(dist, 0)
    max_exact = REL_BUCKETS // 2
    nf = np.maximum(n, 1).astype(np.float32)
    large = max_exact + (np.log(nf / np.float32(max_exact)) / np.float32(math.log(REL_MAX_DIST / max_exact))
                         * np.float32(REL_BUCKETS - max_exact)).astype(np.int32)
    large = np.minimum(large, REL_BUCKETS - 1)
    return np.where(dist < 0, -1, np.where(n < max_exact, n, large)).astype(np.int32)


def _bias_body(tbl_ref, map_ref, o_ref, *, ntile):
    h = pl.program_id(0)
    for t in range(ntile):
        bm = map_ref[t]
        acc = jnp.full(bm.shape, NEG, F32)
        for b in range(REL_BUCKETS):
            acc = jnp.where(bm == b, tbl_ref[h, b], acc)
        o_ref[0, t] = acc


def _bias_tiles(rel_t, maps):
    nhead = rel_t.shape[0]
    ntile, r, c = maps.shape
    return pl.pallas_call(
        functools.partial(_bias_body, ntile=ntile),
        grid=(nhead,),
        in_specs=[
            pl.BlockSpec(memory_space=pltpu.SMEM),
            pl.BlockSpec((ntile, r, c), lambda h: (0, 0, 0)),
        ],
        out_specs=pl.BlockSpec((1, ntile, r, c), lambda h: (h, 0, 0, 0)),
        out_shape=jax.ShapeDtypeStruct((nhead, ntile, r, c), F32),
        compiler_params=_cparams(("arbitrary",)),
    )(rel_t, jnp.asarray(maps))


def _prompt_bias_maps():
    k = np.arange(MOBA_BLOCK)[:, None]
    q = np.arange(MOBA_BLOCK)[None, :]
    return np.stack([_bucket_np(q - k), _bucket_np(MOBA_BLOCK + q - k)])


def _sample_bias_maps(t_new):
    r = np.arange(t_new)[:, None]
    c = np.arange(MOBA_BLOCK)[None, :]
    own = np.where(c < t_new, r - c, -1)
    return np.stack([_bucket_np(MOBA_BLOCK + r - c), _bucket_np(own), _bucket_np(2 * MOBA_BLOCK + r - c + t_new)])


def _topk_select(scores):
    n = len(scores)
    sel = []
    for a in range(n):
        cnt = jnp.zeros(scores[a].shape, F32)
        for b in range(n):
            if b == a:
                continue
            ahead = (scores[b] >= scores[a]) if b < a else (scores[b] > scores[a])
            cnt = cnt + jnp.where(ahead, 1.0, 0.0)
        keep = cnt < MOBA_TOPK
        sel.append(keep)
    return sel


def _moba_p_body(tbl_ref, q_ref, k_ref, v_ref, bias_ref, o_ref, km_sc, kb_sc, vt_sc, lg_sc, *, nblk, dh):
    h = pl.program_id(1)
    j = pl.program_id(2)
    blk = MOBA_BLOCK

    @pl.when(j == 0)
    def _():
        km_sc[...] = jnp.zeros_like(km_sc)
        for b in range(nblk):
            km_sc[b:b + 1, :] = jnp.mean(k_ref[b * blk:(b + 1) * blk, :], axis=0, keepdims=True)
        kb_sc[...] = k_ref[...].astype(BF16)
        vt_sc[...] = v_ref[...].T.astype(BF16)

    q = q_ref[...]
    st = _dot_nt(km_sc[...].astype(BF16), q.astype(BF16))
    rowi = lax.broadcasted_iota(jnp.int32, st.shape, 0)
    st = jnp.where(rowi < j, st, -jnp.inf)
    cnt = jnp.zeros(st.shape, F32)
    for b in range(nblk):
        sb = st[b:b + 1, :]
        cnt = cnt + jnp.where(rowi > b, jnp.where(sb >= st, 1.0, 0.0), jnp.where(sb > st, 1.0, 0.0))
    sel_t = jnp.where(cnt < MOBA_TOPK, jnp.where(rowi < j, 1.0, 0.0), 0.0)

    qs = (q * (dh ** -0.5)).astype(BF16)
    far = tbl_ref[h, REL_BUCKETS - 1]

    def run(jj):
        m = None
        for b in range(jj + 1):
            lg = _dot_nt(kb_sc[b * blk:(b + 1) * blk, :], qs)
            if b == jj:
                lg = lg + bias_ref[0, 0]
            elif b == jj - 1:
                lg = lg + bias_ref[0, 1]
            else:
                lg = lg + far
            if b < jj:
                lg = jnp.where(sel_t[b:b + 1, :] > 0.5, lg, NEG)
            lg_sc[b] = lg
            tmax = jnp.max(lg, axis=0, keepdims=True)
            m = tmax if m is None else jnp.maximum(m, tmax)
        l = jnp.zeros_like(m)
        acc = jnp.zeros((dh, blk), F32)
        for b in range(jj + 1):
            pr = jnp.exp(lg_sc[b] - m)
            l = l + jnp.sum(pr, axis=0, keepdims=True)
            acc = acc + _dot(vt_sc[:, b * blk:(b + 1) * blk], pr.astype(BF16))
        o_ref[...] = (acc / l).T

    for jj in range(nblk):
        pl.when(j == jj)(functools.partial(run, jj))


def _moba_prompt(rel_t, p, bias, *, bsz, seq, colq, colk, colv, nhead, dh):
    assert dh == LANE and seq % MOBA_BLOCK == 0 and MOBA_BLOCK >= REL_MAX_DIST
    nblk = seq // MOBA_BLOCK
    assert nblk <= SUBLANE
    ntile = bias.shape[1]
    return pl.pallas_call(
        functools.partial(_moba_p_body, nblk=nblk, dh=dh),
        grid=(bsz, nhead, nblk),
        in_specs=[
            pl.BlockSpec(memory_space=pltpu.SMEM),
            pl.BlockSpec((MOBA_BLOCK, dh), lambda b, h, j: (b * nblk + j, colq + h)),
            pl.BlockSpec((seq, dh), lambda b, h, j: (b, colk + h)),
            pl.BlockSpec((seq, dh), lambda b, h, j: (b, colv + h)),
            pl.BlockSpec((1, ntile, MOBA_BLOCK, MOBA_BLOCK), lambda b, h, j: (h, 0, 0, 0)),
        ],
        out_specs=pl.BlockSpec((MOBA_BLOCK, dh), lambda b, h, j: (b * nblk + j, h)),
        out_shape=jax.ShapeDtypeStruct((bsz * seq, nhead * dh), F32),
        scratch_shapes=[
            pltpu.VMEM((2 * SUBLANE, dh), F32),
            pltpu.VMEM((seq, dh), BF16),
            pltpu.VMEM((dh, seq), BF16),
            pltpu.VMEM((nblk, MOBA_BLOCK, MOBA_BLOCK), F32),
        ],
        compiler_params=_cparams(("parallel", "parallel", "arbitrary")),
    )(rel_t, p, p, p, bias)


def _moba_s_body(pt_ref, q_ref, kn_ref, vn_ref, bias_ref, *rest, npage, nhead, dh, t_new, page):
    kp = rest[:npage]
    vp = rest[npage:2 * npage]
    o_ref, p_sc, l_sc, acc_sc = rest[2 * npage:]
    ph = pl.program_id(1)
    ppb = MOBA_BLOCK // page
    nblk = npage // ppb

    def heads(x):
        return jnp.stack([x[:, h * dh:(h + 1) * dh] for h in range(nhead)], axis=0)

    def pad_rows(x):
        return jnp.concatenate([x, jnp.zeros((page - t_new, x.shape[1]), x.dtype)], axis=0)

    @pl.when(ph == 0)
    def _():
        q3 = (heads(q_ref[...]) * (dh ** -0.5)).astype(BF16)
        lgs = [jnp.einsum('htd,hkd->htk', q3, kp[pg][0, 0].astype(BF16), preferred_element_type=F32)
               for pg in range(npage)]
        scores = []
        for b in range(nblk):
            tot = lgs[b * ppb]
            for r in range(1, ppb):
                tot = tot + lgs[b * ppb + r]
            scores.append(jnp.sum(tot, axis=-1, keepdims=True))
        sel = _topk_select(scores)
        far = bias_ref[:, 2]
        prev = bias_ref[:, 0]
        ml = []
        for pg in range(npage):
            b = pg // ppb
            r = pg % ppb
            bias = prev[:, :, r * page:(r + 1) * page] if b == nblk - 1 else far[:, :, :page]
            ml.append(jnp.where(sel[b], lgs[pg] + bias, NEG))
        kn3 = heads(pad_rows(kn_ref[...])).astype(BF16)
        vn3 = heads(pad_rows(vn_ref[...])).astype(BF16)
        lo = jnp.einsum('htd,hkd->htk', q3, kn3, preferred_element_type=F32) + bias_ref[:, 1][:, :, :page]
        m = jnp.max(lo, axis=-1, keepdims=True)
        for x in ml:
            m = jnp.maximum(m, jnp.max(x, axis=-1, keepdims=True))
        eo = jnp.exp(lo - m)
        l = jnp.sum(eo, axis=-1, keepdims=True)
        for pg in range(npage):
            e = jnp.exp(ml[pg] - m)
            l = l + jnp.sum(e, axis=-1, keepdims=True)
            p_sc[pg] = e
        l_sc[...] = l
        acc_sc[...] = jnp.einsum('htk,hkd->htd', eo.astype(BF16), vn3, preferred_element_type=F32)

    @pl.when(ph == 1)
    def _():
        acc = acc_sc[...]
        for pg in range(npage):
            acc = acc + jnp.einsum('htk,hkd->htd', p_sc[pg].astype(BF16), vp[pg][0, 0].astype(BF16),
                                   preferred_element_type=F32)
        out = acc / l_sc[...]
        for h in range(nhead):
            o_ref[:, h * dh:(h + 1) * dh] = out[h]


def _moba_sample(p, bias, kpool, vpool, layer, page_table, *, row_off, bsz, t_new, colq, colk, colv, nhead, dh):
    npage = page_table.shape[1]
    page = kpool.shape[3]
    assert dh == LANE and page == LANE and MOBA_BLOCK % page == 0
    assert (npage * page) % MOBA_BLOCK == 0 and npage * page >= MOBA_BLOCK and t_new <= page
    w = nhead * dh
    rb0 = row_off // t_new
    ntile = bias.shape[1]
    pt = page_table.reshape(-1).astype(jnp.int32)
    assert colq % nhead == 0 and colk % nhead == 0 and colv % nhead == 0
    colq, colk, colv = colq // nhead, colk // nhead, colv // nhead

    def kmap(pg):
        return lambda b, ph, pt_ref: (layer, pt_ref[b * npage + pg], 0, 0, 0)

    def vmap_(pg):
        return lambda b, ph, pt_ref: (layer, pt_ref[jnp.maximum(b + ph - 1, 0) * npage + pg], 0, 0, 0)

    in_specs = [
        pl.BlockSpec((t_new, w), lambda b, ph, pt_ref: (rb0 + b, colq)),
        pl.BlockSpec((t_new, w), lambda b, ph, pt_ref: (rb0 + b, colk)),
        pl.BlockSpec((t_new, w), lambda b, ph, pt_ref: (rb0 + b, colv)),
        pl.BlockSpec((nhead, ntile, t_new, MOBA_BLOCK), lambda b, ph, pt_ref: (0, 0, 0, 0)),
    ]
    in_specs += [pl.BlockSpec((1, 1, nhead, page, dh), kmap(pg)) for pg in range(npage)]
    in_specs += [pl.BlockSpec((1, 1, nhead, page, dh), vmap_(pg)) for pg in range(npage)]
    grid_spec = pltpu.PrefetchScalarGridSpec(
        num_scalar_prefetch=1,
        grid=(bsz, 2),
        in_specs=in_specs,
        out_specs=pl.BlockSpec((t_new, w), lambda b, ph, pt_ref: (b, 0)),
        scratch_shapes=[
            pltpu.VMEM((npage, nhead, t_new, page), F32),
            pltpu.VMEM((nhead, t_new, 1), F32),
            pltpu.VMEM((nhead, t_new, dh), F32),
        ],
    )
    return pl.pallas_call(
        functools.partial(_moba_s_body, npage=npage, nhead=nhead, dh=dh, t_new=t_new, page=page),
        grid_spec=grid_spec,
        out_shape=jax.ShapeDtypeStruct((bsz * t_new, w), F32),
        compiler_params=_cparams(("arbitrary", "arbitrary")),
    )(pt, p, p, p, bias, *([kpool] * npage), *([vpool] * npage))


def _mix_body(hm_ref, hb_ref, ga_ref, gb_ref, x_ref, wa_ref, wb_ref, wo_ref, o_ref):
    a = _dot(hm_ref[...].astype(BF16), wa_ref[...])
    b = _dot(hb_ref[...].astype(BF16), wb_ref[...])
    merged = jax.nn.sigmoid(ga_ref[...]) * a + jax.nn.sigmoid(gb_ref[...]) * b
    o_ref[...] = x_ref[...] + _dot(merged.astype(BF16), wo_ref[...])


def _mix(hm, hb, p, x, wa, wb, wo):
    m, d = x.shape
    wm = hm.shape[1]
    wbw = hb.shape[1]
    tm = _pick(m, (256, 128, 64, 32, 16, 8))
    const = dict(pipeline_mode=pl.Buffered(1))
    return pl.pallas_call(
        _mix_body,
        grid=(m // tm,),
        in_specs=[
            pl.BlockSpec((tm, wm), lambda i: (i, 0)),
            pl.BlockSpec((tm, wbw), lambda i: (i, 0)),
            pl.BlockSpec((tm, d), lambda i: (i, 0)),
            pl.BlockSpec((tm, d), lambda i: (i, 1)),
            pl.BlockSpec((tm, d), lambda i: (i, 0)),
            pl.BlockSpec((wm, d), lambda i: (0, 0), **const),
            pl.BlockSpec((wbw, d), lambda i: (0, 0), **const),
            pl.BlockSpec((d, d), lambda i: (0, 0), **const),
        ],
        out_specs=pl.BlockSpec((tm, d), lambda i: (i, 0)),
        out_shape=jax.ShapeDtypeStruct((m, d), F32),
        compiler_params=_cparams(("parallel",)),
    )(hm, hb, p, p, x, wa, wb, wo)


def kernel(x_prompt, x_sample, cache_k, cache_v, state_C, state_n, state_m, page_table, g_ff1, w_ff1_gate,
           w_ff1_up, w_ff1_down, g_mix, w_in, b_ig, b_fg, g_head, w_a, w_b, w_out, g_ff2, w_ff2_gate,
           w_ff2_up, w_ff2_down, rel_bias_table, g_final):
    bp, seq, d = x_prompt.shape
    bd, t_new, _ = x_sample.shape
    depth = g_ff1.shape[0]
    nh_m = b_ig.shape[1]
    w_m = w_a.shape[1]
    dh_m = w_m // nh_m
    nh_b = rel_bias_table.shape[1]
    w_bw = w_b.shape[1]
    dh_b = w_bw // nh_b
    page = cache_k.shape[3]
    mp = bp * seq
    sdt = state_C.dtype
    assert 2 * nh_m <= SUBLANE and (2 * d) % w_m == 0 and (2 * d + 4 * w_m) % LANE == 0
    assert mp % t_new == 0 and seq % MLSTM_PROMPT_CHUNK == 0 and seq % page == 0

    o_gate = 4 * w_m
    o_moba = o_gate + 2 * nh_m
    o_ga = o_moba + 3 * w_bw
    col_m = (2 * d) // w_m
    col_b = (2 * d + 4 * w_m) // dh_b
    colq_b, colk_b, colv_b = col_b, col_b + nh_b, col_b + 2 * nh_b

    rel_t = rel_bias_table.T.astype(F32)
    bias_p = _bias_tiles(rel_t, _prompt_bias_maps())
    bias_s = _bias_tiles(rel_t, _sample_bias_maps(t_new))

    x = jnp.concatenate([x_prompt.reshape(mp, d), x_sample.reshape(bd * t_new, d)], axis=0)
    kp_l, vp_l, ks_l, vs_l = [], [], [], []
    cp_l, np_l, mp_l, cs_l, ns_l, ms_l = [], [], [], [], [], []
    for l in range(depth):
        wi = w_in[l]
        w_main = jnp.concatenate([wi[:, o_ga:], wi[:, :o_gate], wi[:, o_moba:o_ga]], axis=1).astype(BF16)
        w_gate = jnp.pad(wi[:, o_gate:o_moba], ((0, 0), (0, LANE - 2 * nh_m))).astype(BF16)
        gate_bias = jnp.concatenate([b_ig[l], b_fg[l]]).astype(F32)
        bias_c = jnp.pad(gate_bias, (0, LANE - 2 * nh_m)).reshape(1, LANE)
        bias_r = jnp.pad(gate_bias, (0, SUBLANE - 2 * nh_m)).reshape(SUBLANE, 1)

        x = _ffn(x, g_ff1[l], w_ff1_gate[l].astype(BF16), w_ff1_up[l].astype(BF16), w_ff1_down[l].astype(BF16))
        p, gates = _proj(x, g_mix[l], w_main, w_gate)

        gt_p = gates[:mp, :SUBLANE].reshape(bp, seq, SUBLANE).transpose(0, 2, 1)
        gt_s = gates[mp:, :SUBLANE].reshape(bd, t_new, SUBLANE).transpose(0, 2, 1)
        gt_s = jnp.pad(gt_s, ((0, 0), (0, 0), (0, LANE - t_new)))

        mk = dict(col0=col_m, nh=nh_m, dh=dh_m)
        hm_p, (cp, npp, mpp) = _mlstm(p, gates, gt_p, bias_c, bias_r, g_head[l], row_off=0, bsz=bp, seq=seq,
                                      chunk=MLSTM_PROMPT_CHUNK, **mk)
        init = (state_C, state_n, state_m, l)
        hm_s, (cs, ns, ms) = _mlstm(p, gates, gt_s, bias_c, bias_r, g_head[l], row_off=mp, bsz=bd, seq=t_new,
                                    chunk=t_new, init=init, **mk)

        bk = dict(colq=colq_b, colk=colk_b, colv=colv_b, nhead=nh_b, dh=dh_b)
        hb_p = _moba_prompt(rel_t, p, bias_p, bsz=bp, seq=seq, **bk)
        hb_s = _moba_sample(p, bias_s, cache_k, cache_v, l, page_table, row_off=mp, bsz=bd, t_new=t_new, **bk)

        hm = jnp.concatenate([hm_p, hm_s], axis=0)
        hb = jnp.concatenate([hb_p, hb_s], axis=0)
        x = _mix(hm, hb, p, x, w_a[l].astype(BF16), w_b[l].astype(BF16), w_out[l].astype(BF16))
        x = _ffn(x, g_ff2[l], w_ff2_gate[l].astype(BF16), w_ff2_up[l].astype(BF16), w_ff2_down[l].astype(BF16),
                 g_final=g_final if l == depth - 1 else None)

        kcols = p[:, colk_b * dh_b:(colk_b + nh_b) * dh_b]
        vcols = p[:, colv_b * dh_b:(colv_b + nh_b) * dh_b]

        def pages(a):
            return a[:mp].reshape(bp, seq // page, page, nh_b, dh_b).transpose(0, 1, 3, 2, 4)

        def rows(a):
            return a[mp:].reshape(bd, t_new, nh_b, dh_b).transpose(0, 2, 1, 3)

        kp_l.append(pages(kcols))
        vp_l.append(pages(vcols))
        ks_l.append(rows(kcols))
        vs_l.append(rows(vcols))
        cp_l.append(cp.astype(sdt))
        np_l.append(npp.astype(sdt))
        mp_l.append(mpp.astype(sdt))
        cs_l.append(cs.astype(sdt))
        ns_l.append(ns.astype(sdt))
        ms_l.append(ms.astype(sdt))

    y_prompt = x[:mp].reshape(bp, seq, d)
    y_sample = x[mp:].reshape(bd, t_new, d)
    return (y_prompt, y_sample, jnp.stack(kp_l), jnp.stack(vp_l), jnp.stack(ks_l), jnp.stack(vs_l),
            jnp.stack(cp_l), jnp.stack(np_l), jnp.stack(mp_l), jnp.stack(cs_l), jnp.stack(ns_l), jnp.stack(ms_l))
```

```python
import functools
import math

import numpy as np
import jax
import jax.numpy as jnp
from jax import lax
from jax.experimental import pallas as pl
from jax.experimental.pallas import tpu as pltpu

F32 = jnp.float32
BF16 = jnp.bfloat16

NORM_EPS = 1e-6
MOBA_BLOCK = 256
MOBA_TOPK = 3
MOBA_QBLOCK = 128
REL_BUCKETS = 32
REL_MAX_DIST = 128
MLSTM_PROMPT_CHUNK = 256
LANE = 128
SUBLANE = 8
NEG = -1e30
VMEM_LIMIT = 56 * 1024 * 1024


def _pick(n, cands):
    for c in cands:
        if n % c == 0:
            return c
    return n


def _cparams(sem):
    return pltpu.CompilerParams(dimension_semantics=sem, vmem_limit_bytes=VMEM_LIMIT)


def _rms(x, g):
    return x * lax.rsqrt(jnp.mean(x * x, axis=-1, keepdims=True) + NORM_EPS) * g


def _dot(a, b):
    return jnp.dot(a, b, preferred_element_type=F32)


def _dot_nt(a, b):
    return lax.dot_general(a, b, (((1,), (1,)), ((), ())), preferred_element_type=F32)


def _dot_tn(a, b):
    return lax.dot_general(a, b, (((0,), (0,)), ((), ())), preferred_element_type=F32)


def _ffn_body(x_ref, g_ref, wg_ref, wu_ref, wd_ref, *rest, final):
    if final:
        gf_ref, o_ref, hn_ref = rest
    else:
        o_ref, hn_ref = rest
    f = pl.program_id(1)

    @pl.when(f == 0)
    def _():
        hn_ref[...] = _rms(x_ref[...], g_ref[...]).astype(BF16)
        o_ref[...] = jnp.zeros_like(o_ref)

    h = hn_ref[...]
    g = _dot(h, wg_ref[...])
    u = _dot(h, wu_ref[...])
    a = (g * jax.nn.sigmoid(g) * u).astype(BF16)
    o_ref[...] += _dot(a, wd_ref[...])

    @pl.when(f == pl.num_programs(1) - 1)
    def _():
        y = x_ref[...] + 0.5 * o_ref[...]
        if final:
            y = _rms(y, gf_ref[...])
        o_ref[...] = y


def _ffn(x, g, wg, wu, wd, g_final=None):
    m, d = x.shape
    ff = wg.shape[1]
    tm = _pick(m, (512, 256, 128, 64, 32, 16, 8))
    tf = _pick(ff, (512, 256, 128))
    final = g_final is not None
    in_specs = [
        pl.BlockSpec((tm, d), lambda i, f: (i, 0)),
        pl.BlockSpec((1, d), lambda i, f: (0, 0)),
        pl.BlockSpec((d, tf), lambda i, f: (0, f)),
        pl.BlockSpec((d, tf), lambda i, f: (0, f)),
        pl.BlockSpec((tf, d), lambda i, f: (f, 0)),
    ]
    args = [x, g.reshape(1, d), wg, wu, wd]
    if final:
        in_specs.append(pl.BlockSpec((1, d), lambda i, f: (0, 0)))
        args.append(g_final.reshape(1, d))
    return pl.pallas_call(
        functools.partial(_ffn_body, final=final),
        grid=(m // tm, ff // tf),
        in_specs=in_specs,
        out_specs=pl.BlockSpec((tm, d), lambda i, f: (i, 0)),
        out_shape=jax.ShapeDtypeStruct((m, d), F32),
        scratch_shapes=[pltpu.VMEM((tm, d), BF16)],
        compiler_params=_cparams(("parallel", "arbitrary")),
    )(*args)


def _proj_body(x_ref, g_ref, w_ref, wgate_ref, cs_ref, p_ref, tail_ref, gate_ref, hn_ref, *, j_tail):
    j = pl.program_id(1)

    @pl.when(j == 0)
    def _():
        hn = _rms(x_ref[...], g_ref[...]).astype(BF16)
        hn_ref[...] = hn
        gate_ref[...] = _dot(hn, wgate_ref[...])

    acc = _dot(hn_ref[...], w_ref[...])
    p_ref[...] = (acc * cs_ref[...]).astype(BF16)

    @pl.when(j >= j_tail)
    def _():
        tail_ref[...] = acc


def _proj(x, g, w_main, w_gate, col_scale, n_tail):
    m, d = x.shape
    n = w_main.shape[1]
    tm = _pick(m, (1024, 512, 256, 128, 64, 32, 16))
    tn = _pick(math.gcd(n, n_tail), (1024, 512, 256, 128))
    j_tail = (n - n_tail) // tn
    return pl.pallas_call(
        functools.partial(_proj_body, j_tail=j_tail),
        grid=(m // tm, n // tn),
        in_specs=[
            pl.BlockSpec((tm, d), lambda i, j: (i, 0), pipeline_mode=pl.Buffered(1)),
            pl.BlockSpec((1, d), lambda i, j: (0, 0)),
            pl.BlockSpec((d, tn), lambda i, j: (0, j)),
            pl.BlockSpec((d, LANE), lambda i, j: (0, 0)),
            pl.BlockSpec((1, tn), lambda i, j: (0, j)),
        ],
        out_specs=[
            pl.BlockSpec((tm, tn), lambda i, j: (i, j)),
            pl.BlockSpec((tm, tn), lambda i, j: (i, jnp.maximum(j - j_tail, 0))),
            pl.BlockSpec((tm, LANE), lambda i, j: (i, 0)),
        ],
        out_shape=[jax.ShapeDtypeStruct((m, n), BF16), jax.ShapeDtypeStruct((m, n_tail), F32),
                   jax.ShapeDtypeStruct((m, LANE), F32)],
        scratch_shapes=[pltpu.VMEM((tm, d), BF16)],
        compiler_params=_cparams(("parallel", "arbitrary")),
    )(x, g.reshape(1, d), w_main, w_gate, col_scale)


def _log_sigmoid(x):
    return jnp.minimum(x, 0.0) - jnp.log(1.0 + jnp.exp(-jnp.abs(x)))


def _mlstm_body(*refs, nh, dh, lq, lk, has_init):
    if has_init:
        (q_ref, k_ref, v_ref, o_ref, gc_ref, gr_ref, bc_ref, br_ref, gh_ref,
         c0_ref, n0_ref, m0_ref, h_ref, cn_ref, nn_ref, mn_ref, c_sc, n_sc, m_sc) = refs
    else:
        (q_ref, k_ref, v_ref, o_ref, gc_ref, gr_ref, bc_ref, br_ref, gh_ref,
         h_ref, cn_ref, nn_ref, mn_ref, c_sc, n_sc, m_sc) = refs
    c = pl.program_id(1)
    last = pl.num_programs(1) - 1

    @pl.when(c == 0)
    def _():
        if has_init:
            c_sc[...] = c0_ref[0, 0].astype(F32)
            n_sc[...] = n0_ref[0, 0].astype(F32)
            m_sc[...] = m0_ref[0, 0].astype(F32)
        else:
            c_sc[...] = jnp.zeros_like(c_sc)
            n_sc[...] = jnp.zeros_like(n_sc)
            m_sc[...] = jnp.zeros_like(m_sc)

    gcol = gc_ref[...] + bc_ref[...]
    grow = gr_ref[0] + br_ref[...]
    lf_col = _log_sigmoid(gcol)
    lf_row = _log_sigmoid(grow)
    if lq >= LANE:
        ri = lax.broadcasted_iota(jnp.int32, (lq, lq), 0)
        ci = lax.broadcasted_iota(jnp.int32, (lq, lq), 1)
        tril = jnp.where(ci <= ri, 1.0, 0.0).astype(F32)
        triu = jnp.where(ri <= ci, 1.0, 0.0).astype(F32)
        b_col = jnp.dot(tril, lf_col, precision=lax.Precision.HIGHEST, preferred_element_type=F32)
        b_row = jnp.dot(lf_row, triu, precision=lax.Precision.HIGHEST, preferred_element_type=F32)
    else:
        ri = lax.broadcasted_iota(jnp.int32, (lq, LANE), 0)
        ci = lax.broadcasted_iota(jnp.int32, (SUBLANE, lk), 1)
        b_col = jnp.zeros((lq, LANE), F32)
        b_row = jnp.zeros((SUBLANE, lk), F32)
        for s in range(lq):
            b_col = b_col + jnp.where(ri >= s, lf_col[s:s + 1, :], 0.0)
            b_row = b_row + jnp.where(ci >= s, lf_row[:, s:s + 1], 0.0)

    qi = lax.broadcasted_iota(jnp.int32, (lq, lk), 0)
    ki = lax.broadcasted_iota(jnp.int32, (lq, lk), 1)
    causal = ki <= qi
    kvalid = lax.broadcasted_iota(jnp.int32, (lk, 1), 0) < lq
    q_all = q_ref[...]
    k_all = k_ref[...]
    v_all = v_ref[...]
    o_all = o_ref[...]
    gh_all = gh_ref[...]
    m_prev_all = m_sc[...]

    def pad_keys(a):
        if lk == lq:
            return a
        return jnp.concatenate([a, jnp.zeros((lk - lq, a.shape[1]), a.dtype)], axis=0)

    for h in range(nh):
        sl = slice(h * dh, (h + 1) * dh)
        qh = q_all[:, sl].astype(F32)
        kh = pad_keys(k_all[:, sl]).astype(F32)
        vh = pad_keys(v_all[:, sl]).astype(F32)
        bq = b_col[:, nh + h:nh + h + 1]
        ig_k = pad_keys(gcol[:, h:h + 1])
        bk = pad_keys(bq)
        r_row = grow[h:h + 1, :] - b_row[nh + h:nh + h + 1, :]
        m_prev = m_prev_all[:, h:h + 1]
        log_d = jnp.where(causal, bq + r_row, -jnp.inf)
        log_p = bq + m_prev
        m_t = jnp.maximum(log_p, jnp.max(log_d, axis=-1, keepdims=True))
        w_intra = jnp.exp(log_d - m_t)
        w_prev = jnp.exp(log_p - m_t)
        qb = qh.astype(BF16)
        kb = kh.astype(BF16)
        s = _dot_nt(qb, kb) * w_intra
        cmat = c_sc[h]
        nrow = n_sc[h:h + 1, :]
        num = _dot(s.astype(BF16), vh.astype(BF16)) + w_prev * _dot_nt(qb, cmat.astype(BF16))
        den = jnp.sum(s, axis=-1, keepdims=True) + w_prev * jnp.sum(qh * nrow, axis=-1, keepdims=True)
        hh = num / jnp.maximum(jnp.abs(den), jnp.exp(-m_t))
        b_last = bq[lq - 1:lq, :]
        m_last = m_t[lq - 1:lq, :]
        w_last = jnp.where(kvalid, jnp.exp(b_last - bk + ig_k - m_last), 0.0)
        decay = w_prev[lq - 1:lq, :]
        c_new = decay * cmat + _dot_tn((vh * w_last).astype(BF16), kb)
        n_new = decay * nrow + jnp.sum(kh * w_last, axis=0, keepdims=True)
        c_sc[h] = c_new
        n_sc[h:h + 1, :] = n_new
        m_sc[:, h:h + 1] = m_last
        hm = jax.nn.sigmoid(o_all[:, sl].astype(F32)) * hh
        h_ref[:, sl] = _rms(hm, gh_all[:, sl])

    @pl.when(c == last)
    def _():
        cn_ref[0] = c_sc[...]
        nn_ref[0] = n_sc[...]
        mn_ref[0] = m_sc[...]


def _mlstm(p, gates, gates_t, bias_c, bias_r, g_head, *, row_off, bsz, seq, chunk, col0, nh, dh, init=None):
    w = nh * dh
    lq = chunk
    lk = max(chunk, LANE)
    nc = seq // chunk
    rb0 = row_off // lq
    has_init = init is not None

    def rowblk(b, c):
        return rb0 + b * nc + c

    in_specs = [
        pl.BlockSpec((lq, w), lambda b, c: (rowblk(b, c), col0)),
        pl.BlockSpec((lq, w), lambda b, c: (rowblk(b, c), col0 + 1)),
        pl.BlockSpec((lq, w), lambda b, c: (rowblk(b, c), col0 + 2)),
        pl.BlockSpec((lq, w), lambda b, c: (rowblk(b, c), col0 + 3)),
        pl.BlockSpec((lq, LANE), lambda b, c: (rowblk(b, c), 0)),
        pl.BlockSpec((1, SUBLANE, lk), lambda b, c: (b, 0, c)),
        pl.BlockSpec((1, LANE), lambda b, c: (0, 0)),
        pl.BlockSpec((SUBLANE, 1), lambda b, c: (0, 0)),
        pl.BlockSpec((1, w), lambda b, c: (0, 0)),
    ]
    args = [p, p, p, p, gates, gates_t, bias_c, bias_r, g_head.reshape(1, w)]
    if has_init:
        c0, n0, m0, layer = init
        in_specs += [
            pl.BlockSpec((1, 1, nh, dh, dh), lambda b, c: (layer, b, 0, 0, 0)),
            pl.BlockSpec((1, 1, nh, dh), lambda b, c: (layer, b, 0, 0)),
            pl.BlockSpec((1, 1, 1, nh), lambda b, c: (layer, b, 0, 0)),
        ]
        args += [c0, n0, m0.reshape(m0.shape[0], bsz, 1, nh)]
    out_specs = [
        pl.BlockSpec((lq, w), lambda b, c: (b * nc + c, 0)),
        pl.BlockSpec((1, nh, dh, dh), lambda b, c: (b, 0, 0, 0)),
        pl.BlockSpec((1, nh, dh), lambda b, c: (b, 0, 0)),
        pl.BlockSpec((1, 1, nh), lambda b, c: (b, 0, 0)),
    ]
    out_shape = [
        jax.ShapeDtypeStruct((bsz * seq, w), F32),
        jax.ShapeDtypeStruct((bsz, nh, dh, dh), F32),
        jax.ShapeDtypeStruct((bsz, nh, dh), F32),
        jax.ShapeDtypeStruct((bsz, 1, nh), F32),
    ]
    h, cn, nn, mn = pl.pallas_call(
        functools.partial(_mlstm_body, nh=nh, dh=dh, lq=lq, lk=lk, has_init=has_init),
        grid=(bsz, nc),
        in_specs=in_specs,
        out_specs=out_specs,
        out_shape=out_shape,
        scratch_shapes=[pltpu.VMEM((nh, dh, dh), F32), pltpu.VMEM((nh, dh), F32), pltpu.VMEM((1, nh), F32)],
        compiler_params=_cparams(("parallel", "arbitrary")),
    )(*args)
    return h, (cn, nn, mn.reshape(bsz, nh))


def _bucket_np(dist):
    n = np.maximum(dist, 0)
    max_exact = REL_BUCKETS // 2
    nf = np.maximum(n, 1).astype(np.float32)
    large = max_exact + (np.log(nf / np.float32(max_exact)) / np.float32(math.log(REL_MAX_DIST / max_exact))
                         * np.float32(REL_BUCKETS - max_exact)).astype(np.int32)
    large = np.minimum(large, REL_BUCKETS - 1)
    return np.where(dist < 0, -1, np.where(n < max_exact, n, large)).astype(np.int32)


def _bias_body(tbl_ref, map_ref, o_ref, *, ntile):
    h = pl.program_id(0)
    for t in range(ntile):
        bm = map_ref[t]
        acc = jnp.full(bm.shape, NEG, F32)
        for b in range(REL_BUCKETS):
            acc = jnp.where(bm == b, tbl_ref[h, b], acc)
        o_ref[0, t] = acc


def _bias_tiles(rel_t, maps):
    nhead = rel_t.shape[0]
    ntile, r, c = maps.shape
    return pl.pallas_call(
        functools.partial(_bias_body, ntile=ntile),
        grid=(nhead,),
        in_specs=[
            pl.BlockSpec(memory_space=pltpu.SMEM),
            pl.BlockSpec((ntile, r, c), lambda h: (0, 0, 0)),
        ],
        out_specs=pl.BlockSpec((1, ntile, r, c), lambda h: (h, 0, 0, 0)),
        out_shape=jax.ShapeDtypeStruct((nhead, ntile, r, c), F32),
        compiler_params=_cparams(("arbitrary",)),
    )(rel_t, jnp.asarray(maps))


def _prompt_bias_maps():
    k = np.arange(MOBA_BLOCK)[:, None]
    q = np.arange(MOBA_BLOCK)[None, :]
    return np.stack([_bucket_np(q - k), _bucket_np(MOBA_BLOCK + q - k)])


def _sample_bias_maps(t_new):
    r = np.arange(t_new)[:, None]
    c = np.arange(MOBA_BLOCK)[None, :]
    own = np.where(c < t_new, r - c, -1)
    return np.stack([_bucket_np(MOBA_BLOCK + r - c), _bucket_np(own), _bucket_np(2 * MOBA_BLOCK + r - c + t_new)])


def _topk_select(scores):
    n = len(scores)
    sel = []
    for a in range(n):
        cnt = jnp.zeros(scores[a].shape, F32)
        for b in range(n):
            if b == a:
                continue
            ahead = (scores[b] >= scores[a]) if b < a else (scores[b] > scores[a])
            cnt = cnt + jnp.where(ahead, 1.0, 0.0)
        keep = cnt < MOBA_TOPK
        sel.append(keep)
    return sel


def _moba_p_body(tbl_ref, q_ref, k_ref, v_ref, bias_ref, o_ref, km_sc, vt_sc, lg_sc, *, nblk, dh, hps):
    hg = pl.program_id(1)
    j = pl.program_id(2)
    blk = MOBA_BLOCK

    @pl.when(j == 0)
    def _():
        km_sc[...] = jnp.zeros_like(km_sc)
        for hh in range(hps):
            sl = slice(hh * dh, (hh + 1) * dh)
            for b in range(nblk):
                kblk = k_ref[b * blk:(b + 1) * blk, sl].astype(F32)
                km_sc[hh, b:b + 1, :] = jnp.mean(kblk, axis=0, keepdims=True)
            vt_sc[hh] = v_ref[:, sl].astype(F32).T.astype(BF16)

    def select(hh):
        qs = q_ref[:, hh * dh:(hh + 1) * dh]
        st = _dot_nt(km_sc[hh].astype(BF16), qs)
        rowi = lax.broadcasted_iota(jnp.int32, st.shape, 0)
        st = jnp.where(rowi < j, st, -jnp.inf)
        cnt = jnp.zeros(st.shape, F32)
        for b in range(nblk):
            sb = st[b:b + 1, :]
            cnt = cnt + jnp.where(rowi > b, jnp.where(sb >= st, 1.0, 0.0), jnp.where(sb > st, 1.0, 0.0))
        return qs, jnp.where(cnt < MOBA_TOPK, jnp.where(rowi < j, 1.0, 0.0), 0.0)

    picked = [select(hh) for hh in range(hps)]

    def run(jj):
        for hh in range(hps):
            sl = slice(hh * dh, (hh + 1) * dh)
            qs, sel_t = picked[hh]
            far = tbl_ref[hg * hps + hh, REL_BUCKETS - 1]
            m = None
            for b in range(jj + 1):
                lg = _dot_nt(k_ref[b * blk:(b + 1) * blk, sl], qs)
                if b == jj:
                    lg = lg + bias_ref[hh, 0]
                elif b == jj - 1:
                    lg = lg + bias_ref[hh, 1]
                else:
                    lg = lg + far
                if b < jj:
                    lg = jnp.where(sel_t[b:b + 1, :] > 0.5, lg, NEG)
                lg_sc[hh, b] = lg
                tmax = jnp.max(lg, axis=0, keepdims=True)
                m = tmax if m is None else jnp.maximum(m, tmax)
            l = jnp.zeros_like(m)
            acc = jnp.zeros((dh, blk), F32)
            for b in range(jj + 1):
                pr = jnp.exp(lg_sc[hh, b] - m)
                l = l + jnp.sum(pr, axis=0, keepdims=True)
                acc = acc + _dot(vt_sc[hh, :, b * blk:(b + 1) * blk], pr.astype(BF16))
            o_ref[:, sl] = (acc / l).T

    for jj in range(nblk):
        pl.when(j == jj)(functools.partial(run, jj))


def _moba_prompt(rel_t, p, bias, *, bsz, seq, colq, colk, colv, nhead, dh):
    assert dh == LANE and seq % MOBA_BLOCK == 0 and MOBA_BLOCK >= REL_MAX_DIST
    nblk = seq // MOBA_BLOCK
    assert nblk <= SUBLANE
    ntile = bias.shape[1]
    hps = 2
    assert nhead % hps == 0 and colq % hps == 0 and colk % hps == 0 and colv % hps == 0
    cq, ck, cv = colq // hps, colk // hps, colv // hps
    return pl.pallas_call(
        functools.partial(_moba_p_body, nblk=nblk, dh=dh, hps=hps),
        grid=(bsz, nhead // hps, nblk),
        in_specs=[
            pl.BlockSpec(memory_space=pltpu.SMEM),
            pl.BlockSpec((MOBA_BLOCK, hps * dh), lambda b, h, j: (b * nblk + j, cq + h)),
            pl.BlockSpec((seq, hps * dh), lambda b, h, j: (b, ck + h)),
            pl.BlockSpec((seq, hps * dh), lambda b, h, j: (b, cv + h)),
            pl.BlockSpec((hps, ntile, MOBA_BLOCK, MOBA_BLOCK), lambda b, h, j: (h, 0, 0, 0)),
        ],
        out_specs=pl.BlockSpec((MOBA_BLOCK, hps * dh), lambda b, h, j: (b * nblk + j, h)),
        out_shape=jax.ShapeDtypeStruct((bsz * seq, nhead * dh), F32),
        scratch_shapes=[
            pltpu.VMEM((hps, 2 * SUBLANE, dh), F32),
            pltpu.VMEM((hps, dh, seq), BF16),
            pltpu.VMEM((hps, nblk, MOBA_BLOCK, MOBA_BLOCK), F32),
        ],
        compiler_params=_cparams(("parallel", "parallel", "arbitrary")),
    )(rel_t, p, p, p, bias)


def _moba_s_body(pt_ref, q_ref, kn_ref, vn_ref, bias_ref, *rest, npage, nhead, dh, t_new, page, bsz):
    kp = rest[:npage]
    vp = rest[npage:2 * npage]
    o_ref, p_sc, l_sc, acc_sc = rest[2 * npage:]
    s = pl.program_id(0)
    ppb = MOBA_BLOCK // page
    nblk = npage // ppb

    def heads(x):
        return jnp.stack([x[:, h * dh:(h + 1) * dh] for h in range(nhead)], axis=0)

    def pad_rows(x):
        return jnp.concatenate([x, jnp.zeros((page - t_new, x.shape[1]), x.dtype)], axis=0)

    @pl.when(s >= 1)
    def _():
        slot = (s + 1) % 2
        acc = acc_sc[slot]
        for pg in range(npage):
            acc = acc + jnp.einsum('htk,hkd->htd', p_sc[slot, pg].astype(BF16), vp[pg][0, 0].astype(BF16),
                                   preferred_element_type=F32)
        out = acc / l_sc[slot]
        for h in range(nhead):
            o_ref[:, h * dh:(h + 1) * dh] = out[h]

    @pl.when(s < bsz)
    def _():
        slot = s % 2
        q3 = heads(q_ref[...]).astype(BF16)
        lgs =[jnp.einsum('htd,hkd->htk', q3, kp[pg][0, 0].astype(BF16), preferred_element_type=F32)
               for pg in range(npage)]
        scores = []
        for b in range(nblk):
            tot = lgs[b * ppb]
            for r in range(1, ppb):
                tot = tot + lgs[b * ppb + r]
            scores.append(jnp.sum(tot, axis=-1, keepdims=True))
        sel = _topk_select(scores)
        far = bias_ref[:, 2]
        prev = bias_ref[:, 0]
        ml = []
        for pg in range(npage):
            b = pg // ppb
            r = pg % ppb
            bias = prev[:, :, r * page:(r + 1) * page] if b == nblk - 1 else far[:, :, :page]
            ml.append(jnp.where(sel[b], lgs[pg] + bias, NEG))
        kn3 = heads(pad_rows(kn_ref[...])).astype(BF16)
        vn3 = heads(pad_rows(vn_ref[...])).astype(BF16)
        lo = jnp.einsum('htd,hkd->htk', q3, kn3, preferred_element_type=F32) + bias_ref[:, 1][:, :, :page]
        m = jnp.max(lo, axis=-1, keepdims=True)
        for x in ml:
            m = jnp.maximum(m, jnp.max(x, axis=-1, keepdims=True))
        eo = jnp.exp(lo - m)
        l = jnp.sum(eo, axis=-1, keepdims=True)
        for pg in range(npage):
            e = jnp.exp(ml[pg] - m)
            l = l + jnp.sum(e, axis=-1, keepdims=True)
            p_sc[slot, pg] = e
        l_sc[slot] = l
        acc_sc[slot] = jnp.einsum('htk,hkd->htd', eo.astype(BF16), vn3, preferred_element_type=F32)


def _moba_sample(p, bias, kpool, vpool, layer, page_table, *, row_off, bsz, t_new, colq, colk, colv, nhead, dh):
    npage = page_table.shape[1]
    page = kpool.shape[3]
    assert dh == LANE and page == LANE and MOBA_BLOCK % page == 0
    assert (npage * page) % MOBA_BLOCK == 0 and npage * page >= MOBA_BLOCK and t_new <= page
    w = nhead * dh
    rb0 = row_off // t_new
    ntile = bias.shape[1]
    pt = page_table.reshape(-1).astype(jnp.int32)
    assert colq % nhead == 0 and colk % nhead == 0 and colv % nhead == 0
    colq, colk, colv = colq // nhead, colk // nhead, colv // nhead

    def cur(s):
        return jnp.minimum(s, bsz - 1)

    def prev(s):
        return jnp.maximum(s - 1, 0)

    def kmap(pg):
        return lambda s, pt_ref: (layer, pt_ref[cur(s) * npage + pg], 0, 0, 0)

    def vmap_(pg):
        return lambda s, pt_ref: (layer, pt_ref[prev(s) * npage + pg], 0, 0, 0)

    in_specs = [
        pl.BlockSpec((t_new, w), lambda s, pt_ref: (rb0 + cur(s), colq)),
        pl.BlockSpec((t_new, w), lambda s, pt_ref: (rb0 + cur(s), colk)),
        pl.BlockSpec((t_new, w), lambda s, pt_ref: (rb0 + cur(s), colv)),
        pl.BlockSpec((nhead, ntile, t_new, MOBA_BLOCK), lambda s, pt_ref: (0, 0, 0, 0)),
    ]
    in_specs += [pl.BlockSpec((1, 1, nhead, page, dh), kmap(pg)) for pg in range(npage)]
    in_specs += [pl.BlockSpec((1, 1, nhead, page, dh), vmap_(pg)) for pg in range(npage)]
    grid_spec = pltpu.PrefetchScalarGridSpec(
        num_scalar_prefetch=1,
        grid=(bsz + 1,),
        in_specs=in_specs,
        out_specs=pl.BlockSpec((t_new, w), lambda s, pt_ref: (prev(s), 0)),
        scratch_shapes=[
            pltpu.VMEM((2, npage, nhead, t_new, page), F32),
            pltpu.VMEM((2, nhead, t_new, 1), F32),
            pltpu.VMEM((2, nhead, t_new, dh), F32),
        ],
    )
    return pl.pallas_call(
        functools.partial(_moba_s_body, npage=npage, nhead=nhead, dh=dh, t_new=t_new, page=page, bsz=bsz),
        grid_spec=grid_spec,
        out_shape=jax.ShapeDtypeStruct((bsz * t_new, w), F32),
        compiler_params=_cparams(("arbitrary",)),
    )(pt, p, p, p, bias, *([kpool] * npage), *([vpool] * npage))


def _mix_body(hm_ref, hb_ref, ga_ref, gb_ref, x_ref, wa_ref, wb_ref, wo_ref, o_ref):
    a = _dot(hm_ref[...].astype(BF16), wa_ref[...])
    b = _dot(hb_ref[...].astype(BF16), wb_ref[...])
    merged = jax.nn.sigmoid(ga_ref[...].astype(F32)) * a + jax.nn.sigmoid(gb_ref[...].astype(F32)) * b
    o_ref[...] = x_ref[...] + _dot(merged.astype(BF16), wo_ref[...])


def _mix(hm, hb, p, x, wa, wb, wo):
    m, d = x.shape
    wm = hm.shape[1]
    wbw = hb.shape[1]
    tm = _pick(m, (256, 128, 64, 32, 16, 8))
    const = dict(pipeline_mode=pl.Buffered(1))
    return pl.pallas_call(
        _mix_body,
        grid=(m // tm,),
        in_specs=[
            pl.BlockSpec((tm, wm), lambda i: (i, 0)),
            pl.BlockSpec((tm, wbw), lambda i: (i, 0)),
            pl.BlockSpec((tm, d), lambda i: (i, 0)),
            pl.BlockSpec((tm, d), lambda i: (i, 1)),
            pl.BlockSpec((tm, d), lambda i: (i, 0)),
            pl.BlockSpec((wm, d), lambda i: (0, 0), **const),
            pl.BlockSpec((wbw, d), lambda i: (0, 0), **const),
            pl.BlockSpec((d, d), lambda i: (0, 0), **const),
        ],
        out_specs=pl.BlockSpec((tm, d), lambda i: (i, 0)),
        out_shape=jax.ShapeDtypeStruct((m, d), F32),
        compiler_params=_cparams(("parallel",)),
    )(hm, hb, p, p, x, wa, wb, wo)


def kernel(x_prompt, x_sample, cache_k, cache_v, state_C, state_n, state_m, page_table, g_ff1, w_ff1_gate,
           w_ff1_up, w_ff1_down, g_mix, w_in, b_ig, b_fg, g_head, w_a, w_b, w_out, g_ff2, w_ff2_gate,
           w_ff2_up, w_ff2_down, rel_bias_table, g_final):
    bp, seq, d = x_prompt.shape
    bd, t_new, _ = x_sample.shape
    depth = g_ff1.shape[0]
    nh_m = b_ig.shape[1]
    w_m = w_a.shape[1]
    dh_m = w_m // nh_m
    nh_b = rel_bias_table.shape[1]
    w_bw = w_b.shape[1]
    dh_b = w_bw // nh_b
    page = cache_k.shape[3]
    mp = bp * seq
    sdt = state_C.dtype
    assert 2 * nh_m <= SUBLANE and (2 * d) % w_m == 0 and (2 * d + 4 * w_m) % LANE == 0
    assert mp % t_new == 0 and seq % MLSTM_PROMPT_CHUNK == 0 and seq % page == 0

    o_gate = 4 * w_m
    o_moba = o_gate + 2 * nh_m
    o_ga = o_moba + 3 * w_bw
    col_m = (2 * d) // w_m
    col_b = (2 * d + 4 * w_m) // dh_b
    colq_b, colk_b, colv_b = col_b, col_b + nh_b, col_b + 2 * nh_b

    n_slab = 2 * d + 4 * w_m + 3 * w_bw
    col_scale = np.ones((1, n_slab), np.float32)
    col_scale[:, 2 * d + w_m:2 * d + 2 * w_m] = dh_m ** -0.5
    col_scale[:, 2 * d + 4 * w_m:2 * d + 4 * w_m + w_bw] = dh_b ** -0.5
    col_scale = jnp.asarray(col_scale)

    rel_t = rel_bias_table.T.astype(F32)
    bias_p = _bias_tiles(rel_t, _prompt_bias_maps())
    bias_s = _bias_tiles(rel_t, _sample_bias_maps(t_new))

    x = jnp.concatenate([x_prompt.reshape(mp, d), x_sample.reshape(bd * t_new, d)], axis=0)
    kp_l, vp_l, ks_l, vs_l = [], [], [], []
    cp_l, np_l, mp_l, cs_l, ns_l, ms_l = [], [], [], [], [], []
    for l in range(depth):
        wi = w_in[l]
        w_main = jnp.concatenate([wi[:, o_ga:], wi[:, :o_gate], wi[:, o_moba:o_ga]], axis=1).astype(BF16)
        w_gate = jnp.pad(wi[:, o_gate:o_moba], ((0, 0), (0, LANE - 2 * nh_m))).astype(BF16)
        gate_bias = jnp.concatenate([b_ig[l], b_fg[l]]).astype(F32)
        bias_c = jnp.pad(gate_bias, (0, LANE - 2 * nh_m)).reshape(1, LANE)
        bias_r = jnp.pad(gate_bias, (0, SUBLANE - 2 * nh_m)).reshape(SUBLANE, 1)

        x = _ffn(x, g_ff1[l], w_ff1_gate[l].astype(BF16), w_ff1_up[l].astype(BF16), w_ff1_down[l].astype(BF16))
        p, kv, gates = _proj(x, g_mix[l], w_main, w_gate, col_scale, 2 * w_bw)
        p_s = p[mp:].astype(F32)

        gt_p = gates[:mp, :SUBLANE].reshape(bp, seq, SUBLANE).transpose(0, 2, 1)
        gt_s = gates[mp:, :SUBLANE].reshape(bd, t_new, SUBLANE).transpose(0, 2, 1)
        gt_s = jnp.pad(gt_s, ((0, 0), (0, 0), (0, LANE - t_new)))

        mk = dict(col0=col_m, nh=nh_m, dh=dh_m)
        hm_p, (cp, npp, mpp) = _mlstm(p, gates, gt_p, bias_c, bias_r, g_head[l], row_off=0, bsz=bp, seq=seq,
                                      chunk=MLSTM_PROMPT_CHUNK, **mk)
        init = (state_C, state_n, state_m, l)
        hm_s, (cs, ns, ms) = _mlstm(p_s, gates[mp:], gt_s, bias_c, bias_r, g_head[l], row_off=0, bsz=bd, seq=t_new,
                                    chunk=t_new, init=init, **mk)

        bk = dict(colq=colq_b, colk=colk_b, colv=colv_b, nhead=nh_b, dh=dh_b)
        hb_p = _moba_prompt(rel_t, p, bias_p, bsz=bp, seq=seq, **bk)
        hb_s = _moba_sample(p_s, bias_s, cache_k, cache_v, l, page_table, row_off=0, bsz=bd, t_new=t_new, **bk)

        hm = jnp.concatenate([hm_p, hm_s], axis=0)
        hb = jnp.concatenate([hb_p, hb_s], axis=0)
        x = _mix(hm, hb, p, x, w_a[l].astype(BF16), w_b[l].astype(BF16), w_out[l].astype(BF16))
        x = _ffn(x, g_ff2[l], w_ff2_gate[l].astype(BF16), w_ff2_up[l].astype(BF16), w_ff2_down[l].astype(BF16),
                 g_final=g_final if l == depth - 1 else None)

        kcols = kv[:, :w_bw]
        vcols = kv[:, w_bw:]

        def pages(a):
            return a[:mp].reshape(bp, seq // page, page, nh_b, dh_b).transpose(0, 1, 3, 2, 4)

        def rows(a):
            return a[mp:].reshape(bd, t_new, nh_b, dh_b).transpose(0, 2, 1, 3)

        kp_l.append(pages(kcols))
        vp_l.append(pages(vcols))
        ks_l.append(rows(kcols))
        vs_l.append(rows(vcols))
        cp_l.append(cp.astype(sdt))
        np_l.append(npp.astype(sdt))
        mp_l.append(mpp.astype(sdt))
        cs_l.append(cs.astype(sdt))
        ns_l.append(ns.astype(sdt))
        ms_l.append(ms.astype(sdt))

    y_prompt = x[:mp].reshape(bp, seq, d)
    y_sample = x[mp:].reshape(bd, t_new, d)
    return (y_prompt, y_sample, jnp.stack(kp_l), jnp.stack(vp_l), jnp.stack(ks_l), jnp.stack(vs_l),
            jnp.stack(cp_l), jnp.stack(np_l), jnp.stack(mp_l), jnp.stack(cs_l), jnp.stack(ns_l), jnp.stack(ms_l))
```

```python
import functools
import math

import numpy as np
import jax
import jax.numpy as jnp
from jax import lax
from jax.experimental import pallas as pl
from jax.experimental.pallas import tpu as pltpu

F32 = jnp.float32
BF16 = jnp.bfloat16

NORM_EPS = 1e-6
MOBA_BLOCK = 256
MOBA_TOPK = 3
MOBA_QBLOCK = 128
REL_BUCKETS = 32
REL_MAX_DIST = 128
MLSTM_PROMPT_CHUNK = 256
LANE = 128
SUBLANE = 8
NEG = -1e30
VMEM_LIMIT = 56 * 1024 * 1024


def _pick(n, cands):
    for c in cands:
        if n % c == 0:
            return c
    return n


def _cparams(sem):
    return pltpu.CompilerParams(dimension_semantics=sem, vmem_limit_bytes=VMEM_LIMIT)


def _rms(x, g):
    return x * lax.rsqrt(jnp.mean(x * x, axis=-1, keepdims=True) + NORM_EPS) * g


def _dot(a, b):
    return jnp.dot(a, b, preferred_element_type=F32)


def _dot_nt(a, b):
    return lax.dot_general(a, b, (((1,), (1,)), ((), ())), preferred_element_type=F32)


def _dot_tn(a, b):
    return lax.dot_general(a, b, (((0,), (0,)), ((), ())), preferred_element_type=F32)


def _seg_starts(rows, tm):
    starts = [0]
    for r in rows:
        assert r % tm == 0
        starts.append(starts[-1] + r // tm)
    return starts


def _seg_spec(starts, k, tm, width, col=0, ngrid=1):
    lo, n = starts[k], starts[k + 1] - starts[k]
    if ngrid == 1:
        return pl.BlockSpec((tm, width), lambda i: (jnp.clip(i - lo, 0, n - 1), col))
    return pl.BlockSpec((tm, width), lambda i, j: (jnp.clip(i - lo, 0, n - 1), col))


def _seg_load(refs, starts, i):
    x = refs[0][...]
    for k in range(1, len(refs)):
        x = jnp.where(i >= starts[k], refs[k][...], x)
    return x


def _ffn_body(*refs, in_starts, out_starts, final):
    nx, no = len(in_starts) - 1, len(out_starts) - 1
    x_refs = refs[:nx]
    g_ref, wg_ref, wu_ref, wd_ref = refs[nx:nx + 4]
    rest = refs[nx + 4:]
    if final:
        gf_ref, rest = rest[0], rest[1:]
    o_refs = rest[:no]
    hn_ref = rest[no]
    acc_ref = rest[no + 1] if no > 1 else o_refs[0]
    i = pl.program_id(0)
    f = pl.program_id(1)

    @pl.when(f == 0)
    def _():
        hn_ref[...] = _rms(_seg_load(x_refs, in_starts, i), g_ref[...]).astype(BF16)
        acc_ref[...] = jnp.zeros_like(acc_ref)

    h = hn_ref[...]
    g = _dot(h, wg_ref[...])
    u = _dot(h, wu_ref[...])
    a = (g * jax.nn.sigmoid(g) * u).astype(BF16)
    acc_ref[...] += _dot(a, wd_ref[...])

    @pl.when(f == pl.num_programs(1) - 1)
    def _():
        y = _seg_load(x_refs, in_starts, i) + 0.5 * acc_ref[...]
        if final:
            y = _rms(y, gf_ref[...])
        if no == 1:
            o_refs[0][...] = y
        else:
            for k in range(no):
                def store(k=k):
                    o_refs[k][...] = y
                pl.when(jnp.logical_and(i >= out_starts[k], i < out_starts[k + 1]))(store)


def _ffn(xs, g, wg, wu, wd, g_final=None, out_rows=None):
    d = xs[0].shape[1]
    rows_in = [x.shape[0] for x in xs]
    m = sum(rows_in)
    rows_out = list(out_rows) if out_rows is not None else [m]
    ff = wg.shape[1]
    tm = _pick(math.gcd(*rows_in, *rows_out), (512, 256, 128, 64, 32, 16, 8))
    tf = _pick(ff, (512, 256, 128))
    final = g_final is not None
    in_starts = _seg_starts(rows_in, tm)
    out_starts = _seg_starts(rows_out, tm)
    in_specs = [_seg_spec(in_starts, k, tm, d, ngrid=2) for k in range(len(xs))]
    in_specs += [
        pl.BlockSpec((1, d), lambda i, f: (0, 0)),
        pl.BlockSpec((d, tf), lambda i, f: (0, f)),
        pl.BlockSpec((d, tf), lambda i, f: (0, f)),
        pl.BlockSpec((tf, d), lambda i, f: (f, 0)),
    ]
    args = [*xs, g.reshape(1, d), wg, wu, wd]
    if final:
        in_specs.append(pl.BlockSpec((1, d), lambda i, f: (0, 0)))
        args.append(g_final.reshape(1, d))
    scratch = [pltpu.VMEM((tm, d), BF16)]
    if len(rows_out) > 1:
        scratch.append(pltpu.VMEM((tm, d), F32))
    outs = pl.pallas_call(
        functools.partial(_ffn_body, in_starts=in_starts, out_starts=out_starts, final=final),
        grid=(m // tm, ff // tf),
        in_specs=in_specs,
        out_specs=[_seg_spec(out_starts, k, tm, d, ngrid=2) for k in range(len(rows_out))],
        out_shape=[jax.ShapeDtypeStruct((r, d), F32) for r in rows_out],
        scratch_shapes=scratch,
        compiler_params=_cparams(("parallel", "arbitrary")),
    )(*args)
    return outs if out_rows is not None else outs[0]


def _proj_body(x_ref, g_ref, w_ref, wgate_ref, cs_ref, kp_in, vp_in, ks_in, vs_in,
               p_ref, gate_ref, kp_ref, vp_ref, ks_ref, vs_ref, hn_ref, *, j_k, j_v, npt, nhead, dh, page, t_new):
    del kp_in, vp_in, ks_in, vs_in
    i = pl.program_id(0)
    j = pl.program_id(1)
    tm = x_ref.shape[0]

    @pl.when(j == 0)
    def _():
        hn = _rms(x_ref[...], g_ref[...]).astype(BF16)
        hn_ref[...] = hn
        gate_ref[...] = _dot(hn, wgate_ref[...])

    acc = _dot(hn_ref[...], w_ref[...])
    p_ref[...] = (acc * cs_ref[...]).astype(BF16)

    def put(pages_ref, rows_ref):
        @pl.when(i < npt)
        def _():
            for pg in range(tm // page):
                for h in range(nhead):
                    pages_ref[0, pg, h] = acc[pg * page:(pg + 1) * page, h * dh:(h + 1) * dh]

        @pl.when(i >= npt)
        def _():
            for h in range(nhead):
                rows_ref[0, :, h] = acc[:, h * dh:(h + 1) * dh].reshape(tm // t_new, t_new, dh)

    pl.when(j == j_k)(functools.partial(put, kp_ref, ks_ref))
    pl.when(j == j_v)(functools.partial(put, vp_ref, vs_ref))


def _proj(x, g, w_main, w_gate, col_scale, caches, layer, *, mp, nhead, dh, page, t_new):
    m, d = x.shape
    n = w_main.shape[1]
    w = nhead * dh
    tn = w
    tm = _pick(math.gcd(mp, m - mp), (1024, 512, 256, 128))
    assert n % tn == 0 and tm % page == 0 and tm % t_new == 0
    j_k = (n - 2 * w) // tn
    npt = mp // tm
    nst = (m - mp) // tm
    once = dict(pipeline_mode=pl.Buffered(1))
    alias = pl.BlockSpec(memory_space=pl.ANY)
    pages_spec = pl.BlockSpec((1, tm // page, nhead, page, dh),
                              lambda i, j: (layer, jnp.minimum(i, npt - 1), 0, 0, 0), **once)
    rows_spec = pl.BlockSpec((1, tm // t_new, nhead, t_new, dh),
                             lambda i, j: (layer, jnp.clip(i - npt, 0, nst - 1), 0, 0, 0), **once)
    outs = pl.pallas_call(
        functools.partial(_proj_body, j_k=j_k, j_v=j_k + 1, npt=npt, nhead=nhead, dh=dh, page=page, t_new=t_new),
        grid=(m // tm, n // tn),
        in_specs=[
            pl.BlockSpec((tm, d), lambda i, j: (i, 0), **once),
            pl.BlockSpec((1, d), lambda i, j: (0, 0)),
            pl.BlockSpec((d, tn), lambda i, j: (0, j)),
            pl.BlockSpec((d, LANE), lambda i, j: (0, 0)),
            pl.BlockSpec((1, tn), lambda i, j: (0, j)),
            alias, alias, alias, alias,
        ],
        out_specs=[
            pl.BlockSpec((tm, tn), lambda i, j: (i, j)),
            pl.BlockSpec((tm, LANE), lambda i, j: (i, 0)),
            pages_spec, pages_spec, rows_spec, rows_spec,
        ],
        out_shape=[jax.ShapeDtypeStruct((m, n), BF16), jax.ShapeDtypeStruct((m, LANE), F32)]
        + [jax.ShapeDtypeStruct(c.shape, c.dtype) for c in caches],
        input_output_aliases={5: 2, 6: 3, 7: 4, 8: 5},
        scratch_shapes=[pltpu.VMEM((tm, d), BF16)],
        compiler_params=_cparams(("parallel", "arbitrary")),
    )(x, g.reshape(1, d), w_main, w_gate, col_scale, *caches)
    return outs[0], outs[1], tuple(outs[2:])


def _log_sigmoid(x):
    return jnp.minimum(x, 0.0) - jnp.log(1.0 + jnp.exp(-jnp.abs(x)))


def _mlstm_body(*refs, nh, dh, lq, lk, has_init):
    if has_init:
        (q_ref, k_ref, v_ref, o_ref, gc_ref, gr_ref, bc_ref, br_ref, gh_ref,
         c0_ref, n0_ref, m0_ref, c_alias, h_ref, cn_ref, nn_ref, mn_ref, c_sc, n_sc, m_sc) = refs
    else:
        (q_ref, k_ref, v_ref, o_ref, gc_ref, gr_ref, bc_ref, br_ref, gh_ref,
         c_alias, h_ref, cn_ref, nn_ref, mn_ref, c_sc, n_sc, m_sc) = refs
    del c_alias
    c = pl.program_id(1)
    last = pl.num_programs(1) - 1

    @pl.when(c == 0)
    def _():
        if has_init:
            c_sc[...] = c0_ref[0, 0].astype(F32)
            n_sc[...] = n0_ref[0, 0].astype(F32)
            m_sc[...] = m0_ref[0, 0].astype(F32)
        else:
            c_sc[...] = jnp.zeros_like(c_sc)
            n_sc[...] = jnp.zeros_like(n_sc)
            m_sc[...] = jnp.zeros_like(m_sc)

    gcol = gc_ref[...] + bc_ref[...]
    grow = gr_ref[0] + br_ref[...]
    lf_col = _log_sigmoid(gcol)
    lf_row = _log_sigmoid(grow)
    if lq >= LANE:
        ri = lax.broadcasted_iota(jnp.int32, (lq, lq), 0)
        ci = lax.broadcasted_iota(jnp.int32, (lq, lq), 1)
        tril = jnp.where(ci <= ri, 1.0, 0.0).astype(F32)
        triu = jnp.where(ri <= ci, 1.0, 0.0).astype(F32)
        b_col = jnp.dot(tril, lf_col, precision=lax.Precision.HIGHEST, preferred_element_type=F32)
        b_row = jnp.dot(lf_row, triu, precision=lax.Precision.HIGHEST, preferred_element_type=F32)
    else:
        ri = lax.broadcasted_iota(jnp.int32, (lq, LANE), 0)
        ci = lax.broadcasted_iota(jnp.int32, (SUBLANE, lk), 1)
        b_col = jnp.zeros((lq, LANE), F32)
        b_row = jnp.zeros((SUBLANE, lk), F32)
        for s in range(lq):
            b_col = b_col + jnp.where(ri >= s, lf_col[s:s + 1, :], 0.0)
            b_row = b_row + jnp.where(ci >= s, lf_row[:, s:s + 1], 0.0)

    qi = lax.broadcasted_iota(jnp.int32, (lq, lk), 0)
    ki = lax.broadcasted_iota(jnp.int32, (lq, lk), 1)
    causal = ki <= qi
    kvalid = lax.broadcasted_iota(jnp.int32, (lk, 1), 0) < lq
    q_all = q_ref[...]
    k_all = k_ref[...]
    v_all = v_ref[...]
    o_all = o_ref[...]
    gh_all = gh_ref[...]
    m_prev_all = m_sc[...]

    def pad_keys(a):
        if lk == lq:
            return a
        return jnp.concatenate([a, jnp.zeros((lk - lq, a.shape[1]), a.dtype)], axis=0)

    for h in range(nh):
        sl = slice(h * dh, (h + 1) * dh)
        qh = q_all[:, sl].astype(F32)
        kh = pad_keys(k_all[:, sl]).astype(F32)
        vh = pad_keys(v_all[:, sl]).astype(F32)
        bq = b_col[:, nh + h:nh + h + 1]
        ig_k = pad_keys(gcol[:, h:h + 1])
        bk = pad_keys(bq)
        r_row = grow[h:h + 1, :] - b_row[nh + h:nh + h + 1, :]
        m_prev = m_prev_all[:, h:h + 1]
        log_d = jnp.where(causal, bq + r_row, -jnp.inf)
        log_p = bq + m_prev
        m_t = jnp.maximum(log_p, jnp.max(log_d, axis=-1, keepdims=True))
        w_intra = jnp.exp(log_d - m_t)
        w_prev = jnp.exp(log_p - m_t)
        qb = qh.astype(BF16)
        kb = kh.astype(BF16)
        s = _dot_nt(qb, kb) * w_intra
        cmat = c_sc[h]
        nrow = n_sc[h:h + 1, :]
        num = _dot(s.astype(BF16), vh.astype(BF16)) + w_prev * _dot_nt(qb, cmat.astype(BF16))
        den = jnp.sum(s, axis=-1, keepdims=True) + w_prev * jnp.sum(qh * nrow, axis=-1, keepdims=True)
        hh = num / jnp.maximum(jnp.abs(den), jnp.exp(-m_t))
        b_last = bq[lq - 1:lq, :]
        m_last = m_t[lq - 1:lq, :]
        w_last = jnp.where(kvalid, jnp.exp(b_last - bk + ig_k - m_last), 0.0)
        decay = w_prev[lq - 1:lq, :]
        c_new = decay * cmat + _dot_tn((vh * w_last).astype(BF16), kb)
        n_new = decay * nrow + jnp.sum(kh * w_last, axis=0, keepdims=True)
        c_sc[h] = c_new
        n_sc[h:h + 1, :] = n_new
        m_sc[:, h:h + 1] = m_last
        hm = jax.nn.sigmoid(o_all[:, sl].astype(F32)) * hh
        h_ref[:, sl] = _rms(hm, gh_all[:, sl])

    @pl.when(c == last)
    def _():
        cn_ref[0, 0] = c_sc[...]
        nn_ref[0] = n_sc[...]
        mn_ref[0] = m_sc[...]


def _mlstm(p, gates, gates_t, bias_c, bias_r, g_head, c_stack, layer, *, row_off, bsz, seq, chunk, col0, nh, dh,
           init=None):
    w = nh * dh
    lq = chunk
    lk = max(chunk, LANE)
    nc = seq // chunk
    rb0 = row_off // lq
    has_init = init is not None

    def rowblk(b, c):
        return rb0 + b * nc + c

    in_specs = [
        pl.BlockSpec((lq, w), lambda b, c: (rowblk(b, c), col0)),
        pl.BlockSpec((lq, w), lambda b, c: (rowblk(b, c), col0 + 1)),
        pl.BlockSpec((lq, w), lambda b, c: (rowblk(b, c), col0 + 2)),
        pl.BlockSpec((lq, w), lambda b, c: (rowblk(b, c), col0 + 3)),
        pl.BlockSpec((lq, LANE), lambda b, c: (rowblk(b, c), 0)),
        pl.BlockSpec((1, SUBLANE, lk), lambda b, c: (b, 0, c)),
        pl.BlockSpec((1, LANE), lambda b, c: (0, 0)),
        pl.BlockSpec((SUBLANE, 1), lambda b, c: (0, 0)),
        pl.BlockSpec((1, w), lambda b, c: (0, 0)),
    ]
    args = [p, p, p, p, gates, gates_t, bias_c, bias_r, g_head.reshape(1, w)]
    if has_init:
        c0, n0, m0 = init
        in_specs += [
            pl.BlockSpec((1, 1, nh, dh, dh), lambda b, c: (layer, b, 0, 0, 0)),
            pl.BlockSpec((1, 1, nh, dh), lambda b, c: (layer, b, 0, 0)),
            pl.BlockSpec((1, 1, 1, nh), lambda b, c: (layer, b, 0, 0)),
        ]
        args += [c0, n0, m0.reshape(m0.shape[0], bsz, 1, nh)]
    in_specs.append(pl.BlockSpec(memory_space=pl.ANY))
    args.append(c_stack)
    out_specs = [
        pl.BlockSpec((lq, w), lambda b, c: (b * nc + c, 0)),
        pl.BlockSpec((1, 1, nh, dh, dh), lambda b, c: (layer, b, 0, 0, 0)),
        pl.BlockSpec((1, nh, dh), lambda b, c: (b, 0, 0)),
        pl.BlockSpec((1, 1, nh), lambda b, c: (b, 0, 0)),
    ]
    out_shape = [
        jax.ShapeDtypeStruct((bsz * seq, w), F32),
        jax.ShapeDtypeStruct(c_stack.shape, F32),
        jax.ShapeDtypeStruct((bsz, nh, dh), F32),
        jax.ShapeDtypeStruct((bsz, 1, nh), F32),
    ]
    h, cn, nn, mn = pl.pallas_call(
        functools.partial(_mlstm_body, nh=nh, dh=dh, lq=lq, lk=lk, has_init=has_init),
        grid=(bsz, nc),
        in_specs=in_specs,
        out_specs=out_specs,
        out_shape=out_shape,
        input_output_aliases={len(args) - 1: 1},
        scratch_shapes=[pltpu.VMEM((nh, dh, dh), F32), pltpu.VMEM((nh, dh), F32), pltpu.VMEM((1, nh), F32)],
        compiler_params=_cparams(("parallel", "arbitrary")),
    )(*args)
    return h, (cn, nn, mn.reshape(bsz, nh))


def _bucket_np(dist):
    n = np.maximum(dist, 0)
    max_exact = REL_BUCKETS // 2
    nf = np.maximum(n, 1).astype(np.float32)
    large = max_exact + (np.log(nf / np.float32(max_exact)) / np.float32(math.log(REL_MAX_DIST / max_exact))
                         * np.float32(REL_BUCKETS - max_exact)).astype(np.int32)
    large = np.minimum(large, REL_BUCKETS - 1)
    return np.where(dist < 0, -1, np.where(n < max_exact, n, large)).astype(np.int32)


def _bias_body(tbl_ref, map_ref, o_ref, *, ntile):
    h = pl.program_id(0)
    for t in range(ntile):
        bm = map_ref[t]
        acc = jnp.full(bm.shape, NEG, F32)
        for b in range(REL_BUCKETS):
            acc = jnp.where(bm == b, tbl_ref[h, b], acc)
        o_ref[0, t] = acc


def _bias_tiles(rel_t, maps):
    nhead = rel_t.shape[0]
    ntile, r, c = maps.shape
    return pl.pallas_call(
        functools.partial(_bias_body, ntile=ntile),
        grid=(nhead,),
        in_specs=[
            pl.BlockSpec(memory_space=pltpu.SMEM),
            pl.BlockSpec((ntile, r, c), lambda h: (0, 0, 0)),
        ],
        out_specs=pl.BlockSpec((1, ntile, r, c), lambda h: (h, 0, 0, 0)),
        out_shape=jax.ShapeDtypeStruct((nhead, ntile, r, c), F32),
        compiler_params=_cparams(("arbitrary",)),
    )(rel_t, jnp.asarray(maps))


def _prompt_bias_maps():
    k = np.arange(MOBA_BLOCK)[:, None]
    q = np.arange(MOBA_BLOCK)[None, :]
    return np.stack([_bucket_np(q - k), _bucket_np(MOBA_BLOCK + q - k)])


def _sample_bias_maps(t_new):
    r = np.arange(t_new)[:, None]
    c = np.arange(MOBA_BLOCK)[None, :]
    own = np.where(c < t_new, r - c, -1)
    return np.stack([_bucket_np(MOBA_BLOCK + r - c), _bucket_np(own), _bucket_np(2 * MOBA_BLOCK + r - c + t_new)])


def _topk_select(scores):
    n = len(scores)
    sel = []
    for a in range(n):
        cnt = jnp.zeros(scores[a].shape, F32)
        for b in range(n):
            if b == a:
                continue
            ahead = (scores[b] >= scores[a]) if b < a else (scores[b] > scores[a])
            cnt = cnt + jnp.where(ahead, 1.0, 0.0)
        keep = cnt < MOBA_TOPK
        sel.append(keep)
    return sel


def _moba_p_body(tbl_ref, q_ref, k_ref, v_ref, bias_ref, o_ref, km_sc, vt_sc, lg_sc, *, nblk, dh, hps):
    hg = pl.program_id(1)
    j = pl.program_id(2)
    blk = MOBA_BLOCK

    @pl.when(j == 0)
    def _():
        km_sc[...] = jnp.zeros_like(km_sc)
        for hh in range(hps):
            sl = slice(hh * dh, (hh + 1) * dh)
            for b in range(nblk):
                kblk = k_ref[b * blk:(b + 1) * blk, sl].astype(F32)
                km_sc[hh, b:b + 1, :] = jnp.mean(kblk, axis=0, keepdims=True)
            vt_sc[hh] = v_ref[:, sl].astype(F32).T.astype(BF16)

    def select(hh):
        qs = q_ref[:, hh * dh:(hh + 1) * dh]
        st = _dot_nt(km_sc[hh].astype(BF16), qs)
        rowi = lax.broadcasted_iota(jnp.int32, st.shape, 0)
        st = jnp.where(rowi < j, st, -jnp.inf)
        cnt = jnp.zeros(st.shape, F32)
        for b in range(nblk):
            sb = st[b:b + 1, :]
            cnt = cnt + jnp.where(rowi > b, jnp.where(sb >= st, 1.0, 0.0), jnp.where(sb > st, 1.0, 0.0))
        return qs, jnp.where(cnt < MOBA_TOPK, jnp.where(rowi < j, 1.0, 0.0), 0.0)

    picked = [select(hh) for hh in range(hps)]

    def run(jj):
        for hh in range(hps):
            sl = slice(hh * dh, (hh + 1) * dh)
            qs, sel_t = picked[hh]
            far = tbl_ref[hg * hps + hh, REL_BUCKETS - 1]
            m = None
            for b in range(jj + 1):
                lg = _dot_nt(k_ref[b * blk:(b + 1) * blk, sl], qs)
                if b == jj:
                    lg = lg + bias_ref[hh, 0]
                elif b == jj - 1:
                    lg = lg + bias_ref[hh, 1]
                else:
                    lg = lg + far
                if b < jj:
                    lg = jnp.where(sel_t[b:b + 1, :] > 0.5, lg, NEG)
                lg_sc[hh, b] = lg
                tmax = jnp.max(lg, axis=0, keepdims=True)
                m = tmax if m is None else jnp.maximum(m, tmax)
            l = jnp.zeros_like(m)
            acc = jnp.zeros((dh, blk), F32)
            for b in range(jj + 1):
                pr = jnp.exp(lg_sc[hh, b] - m)
                l = l + jnp.sum(pr, axis=0, keepdims=True)
                acc = acc + _dot(vt_sc[hh, :, b * blk:(b + 1) * blk], pr.astype(BF16))
            o_ref[:, sl] = (acc / l).T

    for jj in range(nblk):
        pl.when(j == jj)(functools.partial(run, jj))


def _moba_prompt(rel_t, p, bias, *, bsz, seq, colq, colk, colv, nhead, dh):
    assert dh == LANE and seq % MOBA_BLOCK == 0 and MOBA_BLOCK >= REL_MAX_DIST
    nblk = seq // MOBA_BLOCK
    assert nblk <= SUBLANE
    ntile = bias.shape[1]
    hps = 2
    assert nhead % hps == 0 and colq % hps == 0 and colk % hps == 0 and colv % hps == 0
    cq, ck, cv = colq // hps, colk // hps, colv // hps
    return pl.pallas_call(
        functools.partial(_moba_p_body, nblk=nblk, dh=dh, hps=hps),
        grid=(bsz, nhead // hps, nblk),
        in_specs=[
            pl.BlockSpec(memory_space=pltpu.SMEM),
            pl.BlockSpec((MOBA_BLOCK, hps * dh), lambda b, h, j: (b * nblk + j, cq + h)),
            pl.BlockSpec((seq, hps * dh), lambda b, h, j: (b, ck + h)),
            pl.BlockSpec((seq, hps * dh), lambda b, h, j: (b, cv + h)),
            pl.BlockSpec((hps, ntile, MOBA_BLOCK, MOBA_BLOCK), lambda b, h, j: (h, 0, 0, 0)),
        ],
        out_specs=pl.BlockSpec((MOBA_BLOCK, hps * dh), lambda b, h, j: (b * nblk + j, h)),
        out_shape=jax.ShapeDtypeStruct((bsz * seq, nhead * dh), F32),
        scratch_shapes=[
            pltpu.VMEM((hps, 2 * SUBLANE, dh), F32),
            pltpu.VMEM((hps, dh, seq), BF16),
            pltpu.VMEM((hps, nblk, MOBA_BLOCK, MOBA_BLOCK), F32),
        ],
        compiler_params=_cparams(("parallel", "parallel", "arbitrary")),
    )(rel_t, p, p, p, bias)


def _moba_s_body(pt_ref, q_ref, kn_ref, vn_ref, bias_ref, *rest, npage, nhead, dh, t_new, page, bsz):
    kp = rest[:npage]
    vp = rest[npage:2 * npage]
    o_ref, p_sc, l_sc, acc_sc = rest[2 * npage:]
    s = pl.program_id(0)
    ppb = MOBA_BLOCK // page
    nblk = npage // ppb

    def heads(x):
        return jnp.stack([x[:, h * dh:(h + 1) * dh] for h in range(nhead)], axis=0)

    def pad_rows(x):
        return jnp.concatenate([x, jnp.zeros((page - t_new, x.shape[1]), x.dtype)], axis=0)

    @pl.when(s >= 1)
    def _():
        slot = (s + 1) % 2
        acc = acc_sc[slot]
        for pg in range(npage):
            acc = acc + jnp.einsum('htk,hkd->htd', p_sc[slot, pg].astype(BF16), vp[pg][0, 0].astype(BF16),
                                   preferred_element_type=F32)
        out = acc / l_sc[slot]
        for h in range(nhead):
            o_ref[:, h * dh:(h + 1) * dh] = out[h]

    @pl.when(s < bsz)
    def _():
        slot = s % 2
        q3 = heads(q_ref[...]).astype(BF16)
        lgs =[jnp.einsum('htd,hkd->htk', q3, kp[pg][0, 0].astype(BF16), preferred_element_type=F32)
               for pg in range(npage)]
        scores = []
        for b in range(nblk):
            tot = lgs[b * ppb]
            for r in range(1, ppb):
                tot = tot + lgs[b * ppb + r]
            scores.append(jnp.sum(tot, axis=-1, keepdims=True))
        sel = _topk_select(scores)
        far = bias_ref[:, 2]
        prev = bias_ref[:, 0]
        ml = []
        for pg in range(npage):
            b = pg // ppb
            r = pg % ppb
            bias = prev[:, :, r * page:(r + 1) * page] if b == nblk - 1 else far[:, :, :page]
            ml.append(jnp.where(sel[b], lgs[pg] + bias, NEG))
        kn3 = heads(pad_rows(kn_ref[...])).astype(BF16)
        vn3 = heads(pad_rows(vn_ref[...])).astype(BF16)
        lo = jnp.einsum('htd,hkd->htk', q3, kn3, preferred_element_type=F32) + bias_ref[:, 1][:, :, :page]
        m = jnp.max(lo, axis=-1, keepdims=True)
        for x in ml:
            m = jnp.maximum(m, jnp.max(x, axis=-1, keepdims=True))
        eo = jnp.exp(lo - m)
        l = jnp.sum(eo, axis=-1, keepdims=True)
        for pg in range(npage):
            e = jnp.exp(ml[pg] - m)
            l = l + jnp.sum(e, axis=-1, keepdims=True)
            p_sc[slot, pg] = e
        l_sc[slot] = l
        acc_sc[slot] = jnp.einsum('htk,hkd->htd', eo.astype(BF16), vn3, preferred_element_type=F32)


def _moba_sample(p, bias, kpool, vpool, layer, page_table, *, row_off, bsz, t_new, colq, colk, colv, nhead, dh):
    npage = page_table.shape[1]
    page = kpool.shape[3]
    assert dh == LANE and page == LANE and MOBA_BLOCK % page == 0
    assert (npage * page) % MOBA_BLOCK == 0 and npage * page >= MOBA_BLOCK and t_new <= page
    w = nhead * dh
    rb0 = row_off // t_new
    ntile = bias.shape[1]
    pt = page_table.reshape(-1).astype(jnp.int32)
    assert colq % nhead == 0 and colk % nhead == 0 and colv % nhead == 0
    colq, colk, colv = colq // nhead, colk // nhead, colv // nhead

    def cur(s):
        return jnp.minimum(s, bsz - 1)

    def prev(s):
        return jnp.maximum(s - 1, 0)

    def kmap(pg):
        return lambda s, pt_ref: (layer, pt_ref[cur(s) * npage + pg], 0, 0, 0)

    def vmap_(pg):
        return lambda s, pt_ref: (layer, pt_ref[prev(s) * npage + pg], 0, 0, 0)

    in_specs = [
        pl.BlockSpec((t_new, w), lambda s, pt_ref: (rb0 + cur(s), colq)),
        pl.BlockSpec((t_new, w), lambda s, pt_ref: (rb0 + cur(s), colk)),
        pl.BlockSpec((t_new, w), lambda s, pt_ref: (rb0 + cur(s), colv)),
        pl.BlockSpec((nhead, ntile, t_new, MOBA_BLOCK), lambda s, pt_ref: (0, 0, 0, 0)),
    ]
    in_specs += [pl.BlockSpec((1, 1, nhead, page, dh), kmap(pg)) for pg in range(npage)]
    in_specs += [pl.BlockSpec((1, 1, nhead, page, dh), vmap_(pg)) for pg in range(npage)]
    grid_spec = pltpu.PrefetchScalarGridSpec(
        num_scalar_prefetch=1,
        grid=(bsz + 1,),
        in_specs=in_specs,
        out_specs=pl.BlockSpec((t_new, w), lambda s, pt_ref: (prev(s), 0)),
        scratch_shapes=[
            pltpu.VMEM((2, npage, nhead, t_new, page), F32),
            pltpu.VMEM((2, nhead, t_new, 1), F32),
            pltpu.VMEM((2, nhead, t_new, dh), F32),
        ],
    )
    return pl.pallas_call(
        functools.partial(_moba_s_body, npage=npage, nhead=nhead, dh=dh, t_new=t_new, page=page, bsz=bsz),
        grid_spec=grid_spec,
        out_shape=jax.ShapeDtypeStruct((bsz * t_new, w), F32),
        compiler_params=_cparams(("arbitrary",)),
    )(pt, p, p, p, bias, *([kpool] * npage), *([vpool] * npage))


def _mix_body(*refs, starts):
    ns = len(starts) - 1
    hm_refs, hb_refs = refs[:ns], refs[ns:2 * ns]
    ga_ref, gb_ref, x_ref, wa_ref, wb_ref, wo_ref, o_ref = refs[2 * ns:]
    i = pl.program_id(0)
    a = _dot(_seg_load(hm_refs, starts, i).astype(BF16), wa_ref[...])
    b = _dot(_seg_load(hb_refs, starts, i).astype(BF16), wb_ref[...])
    merged = jax.nn.sigmoid(ga_ref[...].astype(F32)) * a + jax.nn.sigmoid(gb_ref[...].astype(F32)) * b
    o_ref[...] = x_ref[...] + _dot(merged.astype(BF16), wo_ref[...])


def _mix(hms, hbs, p, x, wa, wb, wo):
    m, d = x.shape
    wm = hms[0].shape[1]
    wbw = hbs[0].shape[1]
    rows = [h.shape[0] for h in hms]
    assert rows == [h.shape[0] for h in hbs] and sum(rows) == m
    tm = _pick(math.gcd(*rows), (256, 128, 64, 32, 16, 8))
    starts = _seg_starts(rows, tm)
    const = dict(pipeline_mode=pl.Buffered(1))
    return pl.pallas_call(
        functools.partial(_mix_body, starts=starts),
        grid=(m // tm,),
        in_specs=[_seg_spec(starts, k, tm, wm) for k in range(len(rows))]
        + [_seg_spec(starts, k, tm, wbw) for k in range(len(rows))]
        + [
            pl.BlockSpec((tm, d), lambda i: (i, 0)),
            pl.BlockSpec((tm, d), lambda i: (i, 1)),
            pl.BlockSpec((tm, d), lambda i: (i, 0)),
            pl.BlockSpec((wm, d), lambda i: (0, 0), **const),
            pl.BlockSpec((wbw, d), lambda i: (0, 0), **const),
            pl.BlockSpec((d, d), lambda i: (0, 0), **const),
        ],
        out_specs=pl.BlockSpec((tm, d), lambda i: (i, 0)),
        out_shape=jax.ShapeDtypeStruct((m, d), F32),
        compiler_params=_cparams(("parallel",)),
    )(*hms, *hbs, p, p, x, wa, wb, wo)


def kernel(x_prompt, x_sample, cache_k, cache_v, state_C, state_n, state_m, page_table, g_ff1, w_ff1_gate,
           w_ff1_up, w_ff1_down, g_mix, w_in, b_ig, b_fg, g_head, w_a, w_b, w_out, g_ff2, w_ff2_gate,
           w_ff2_up, w_ff2_down, rel_bias_table, g_final):
    bp, seq, d = x_prompt.shape
    bd, t_new, _ = x_sample.shape
    depth = g_ff1.shape[0]
    nh_m = b_ig.shape[1]
    w_m = w_a.shape[1]
    dh_m = w_m // nh_m
    nh_b = rel_bias_table.shape[1]
    w_bw = w_b.shape[1]
    dh_b = w_bw // nh_b
    page = cache_k.shape[3]
    mp = bp * seq
    sdt = state_C.dtype
    assert 2 * nh_m <= SUBLANE and (2 * d) % w_m == 0 and (2 * d + 4 * w_m) % LANE == 0
    assert mp % t_new == 0 and seq % MLSTM_PROMPT_CHUNK == 0 and seq % page == 0

    o_gate = 4 * w_m
    o_moba = o_gate + 2 * nh_m
    o_ga = o_moba + 3 * w_bw
    col_m = (2 * d) // w_m
    col_b = (2 * d + 4 * w_m) // dh_b
    colq_b, colk_b, colv_b = col_b, col_b + nh_b, col_b + 2 * nh_b

    n_slab = 2 * d + 4 * w_m + 3 * w_bw
    col_scale = np.ones((1, n_slab), np.float32)
    col_scale[:, 2 * d + w_m:2 * d + 2 * w_m] = dh_m ** -0.5
    col_scale[:, 2 * d + 4 * w_m:2 * d + 4 * w_m + w_bw] = dh_b ** -0.5
    col_scale = jnp.asarray(col_scale)

    rel_t = rel_bias_table.T.astype(F32)
    bias_p = _bias_tiles(rel_t, _prompt_bias_maps())
    bias_s = _bias_tiles(rel_t, _sample_bias_maps(t_new))

    ms_rows = bd * t_new
    xs = [x_prompt.reshape(mp, d), x_sample.reshape(ms_rows, d)]
    caches = (jnp.zeros((depth, mp // page, nh_b, page, dh_b), F32), jnp.zeros((depth, mp // page, nh_b, page, dh_b), F32),
              jnp.zeros((depth, bd, nh_b, t_new, dh_b), F32), jnp.zeros((depth, bd, nh_b, t_new, dh_b), F32))
    c_prompt = jnp.zeros((depth, bp, nh_m, dh_m, dh_m), F32)
    c_sample = jnp.zeros((depth, bd, nh_m, dh_m, dh_m), F32)
    np_l, mp_l, ns_l, ms_l = [], [], [], []
    for l in range(depth):
        wi = w_in[l]
        w_main = jnp.concatenate([wi[:, o_ga:], wi[:, :o_gate], wi[:, o_moba:o_ga]], axis=1).astype(BF16)
        w_gate = jnp.pad(wi[:, o_gate:o_moba], ((0, 0), (0, LANE - 2 * nh_m))).astype(BF16)
        gate_bias = jnp.concatenate([b_ig[l], b_fg[l]]).astype(F32)
        bias_c = jnp.pad(gate_bias, (0, LANE - 2 * nh_m)).reshape(1, LANE)
        bias_r = jnp.pad(gate_bias, (0, SUBLANE - 2 * nh_m)).reshape(SUBLANE, 1)

        x = _ffn(xs, g_ff1[l], w_ff1_gate[l].astype(BF16), w_ff1_up[l].astype(BF16), w_ff1_down[l].astype(BF16))
        p, gates, caches = _proj(x, g_mix[l], w_main, w_gate, col_scale, caches, l, mp=mp, nhead=nh_b, dh=dh_b,
                                 page=page, t_new=t_new)
        p_s = p[mp:].astype(F32)

        gt_p = gates[:mp, :SUBLANE].reshape(bp, seq, SUBLANE).transpose(0, 2, 1)
        gt_s = gates[mp:, :SUBLANE].reshape(bd, t_new, SUBLANE).transpose(0, 2, 1)
        gt_s = jnp.pad(gt_s, ((0, 0), (0, 0), (0, LANE - t_new)))

        mk = dict(col0=col_m, nh=nh_m, dh=dh_m)
        hm_p, (c_prompt, npp, mpp) = _mlstm(p, gates, gt_p, bias_c, bias_r, g_head[l], c_prompt, l, row_off=0,
                                            bsz=bp, seq=seq, chunk=MLSTM_PROMPT_CHUNK, **mk)
        init = (state_C, state_n, state_m)
        hm_s, (c_sample, ns, ms) = _mlstm(p_s, gates[mp:], gt_s, bias_c, bias_r, g_head[l], c_sample, l, row_off=0,
                                          bsz=bd, seq=t_new, chunk=t_new, init=init, **mk)

        bk = dict(colq=colq_b, colk=colk_b, colv=colv_b, nhead=nh_b, dh=dh_b)
        hb_p = _moba_prompt(rel_t, p, bias_p, bsz=bp, seq=seq, **bk)
        hb_s = _moba_sample(p_s, bias_s, cache_k, cache_v, l, page_table, row_off=0, bsz=bd, t_new=t_new, **bk)

        x = _mix([hm_p, hm_s], [hb_p, hb_s], p, x, w_a[l].astype(BF16), w_b[l].astype(BF16), w_out[l].astype(BF16))
        ffn2 = (g_ff2[l], w_ff2_gate[l].astype(BF16), w_ff2_up[l].astype(BF16), w_ff2_down[l].astype(BF16))
        if l == depth - 1:
            y_prompt, y_sample = _ffn([x], *ffn2, g_final=g_final, out_rows=(mp, ms_rows))
        else:
            xs = [_ffn([x], *ffn2)]

        np_l.append(npp.astype(sdt))
        mp_l.append(mpp.astype(sdt))
        ns_l.append(ns.astype(sdt))
        ms_l.append(ms.astype(sdt))

    k_pages, v_pages, k_rows, v_rows = caches
    page_shape = (depth, bp, seq // page, nh_b, page, dh_b)
    return (y_prompt.reshape(bp, seq, d), y_sample.reshape(bd, t_new, d),
            k_pages.reshape(page_shape), v_pages.reshape(page_shape), k_rows, v_rows,
            c_prompt.astype(sdt), jnp.stack(np_l), jnp.stack(mp_l), c_sample.astype(sdt), jnp.stack(ns_l), jnp.stack(ms_l))
```

```python
import functools
import math

import numpy as np
import jax
import jax.numpy as jnp
from jax import lax
from jax.experimental import pallas as pl
from jax.experimental.pallas import tpu as pltpu

F32 = jnp.float32
BF16 = jnp.bfloat16

NORM_EPS = 1e-6
MOBA_BLOCK = 256
MOBA_TOPK = 3
MOBA_QBLOCK = 128
REL_BUCKETS = 32
REL_MAX_DIST = 128
MLSTM_PROMPT_CHUNK = 256
LANE = 128
SUBLANE = 8
NEG = -1e30
VMEM_LIMIT = 56 * 1024 * 1024


def _pick(n, cands):
    for c in cands:
        if n % c == 0:
            return c
    return n


def _cparams(sem):
    return pltpu.CompilerParams(dimension_semantics=sem, vmem_limit_bytes=VMEM_LIMIT)


def _rms(x, g):
    return x * lax.rsqrt(jnp.mean(x * x, axis=-1, keepdims=True) + NORM_EPS) * g


def _dot(a, b):
    return jnp.dot(a, b, preferred_element_type=F32)


def _dot_nt(a, b):
    return lax.dot_general(a, b, (((1,), (1,)), ((), ())), preferred_element_type=F32)


def _dot_tn(a, b):
    return lax.dot_general(a, b, (((0,), (0,)), ((), ())), preferred_element_type=F32)


CAST_BLOCK_BYTES = 6 * 1024 * 1024


def _cast_body(w_ref, o_ref):
    o_ref[...] = w_ref[0].astype(o_ref.dtype)


def _cast_layer(w, layer):
    _, rows, cols = w.shape
    tr = rows
    while tr % 2 == 0 and tr // 2 >= 2 * SUBLANE and tr * cols * 4 > CAST_BLOCK_BYTES:
        tr //= 2
    return pl.pallas_call(
        _cast_body,
        grid=(rows // tr,),
        in_specs=[pl.BlockSpec((1, tr, cols), lambda i: (layer, i, 0))],
        out_specs=pl.BlockSpec((tr, cols), lambda i: (i, 0)),
        out_shape=jax.ShapeDtypeStruct((rows, cols), BF16),
        compiler_params=_cparams(("parallel",)),
    )(w)


def _seg_starts(rows, tm):
    starts = [0]
    for r in rows:
        assert r % tm == 0
        starts.append(starts[-1] + r // tm)
    return starts


def _seg_spec(starts, k, tm, width, col=0, ngrid=1, **kw):
    lo, n = starts[k], starts[k + 1] - starts[k]
    if ngrid == 1:
        return pl.BlockSpec((tm, width), lambda i: (jnp.clip(i - lo, 0, n - 1), col), **kw)
    return pl.BlockSpec((tm, width), lambda i, j: (jnp.clip(i - lo, 0, n - 1), col), **kw)


def _seg_load(refs, starts, i):
    x = refs[0][...]
    for k in range(1, len(refs)):
        x = jnp.where(i >= starts[k], refs[k][...], x)
    return x


def _ffn_body(*refs, in_starts, out_starts, final):
    nx, no = len(in_starts) - 1, len(out_starts) - 1
    x_refs = refs[:nx]
    g_ref, wg_ref, wu_ref, wd_ref = refs[nx:nx + 4]
    rest = refs[nx + 4:]
    if final:
        gf_ref, rest = rest[0], rest[1:]
    o_refs = rest[:no]
    hn_ref = rest[no]
    acc_ref = rest[no + 1] if no > 1 else o_refs[0]
    i = pl.program_id(0)
    f = pl.program_id(1)

    @pl.when(f == 0)
    def _():
        hn_ref[...] = _rms(_seg_load(x_refs, in_starts, i), g_ref[...]).astype(BF16)
        acc_ref[...] = jnp.zeros_like(acc_ref)

    h = hn_ref[...]
    g = _dot(h, wg_ref[...])
    u = _dot(h, wu_ref[...])
    a = (g * jax.nn.sigmoid(g) * u).astype(BF16)
    acc_ref[...] += _dot(a, wd_ref[...])

    @pl.when(f == pl.num_programs(1) - 1)
    def _():
        y = _seg_load(x_refs, in_starts, i) + 0.5 * acc_ref[...]
        if final:
            y = _rms(y, gf_ref[...])
        if no == 1:
            o_refs[0][...] = y
        else:
            for k in range(no):
                def store(k=k):
                    o_refs[k][...] = y
                pl.when(jnp.logical_and(i >= out_starts[k], i < out_starts[k + 1]))(store)


def _ffn(xs, g, wg, wu, wd, g_final=None, out_rows=None):
    d = xs[0].shape[1]
    rows_in = [x.shape[0] for x in xs]
    m = sum(rows_in)
    rows_out = list(out_rows) if out_rows is not None else [m]
    ff = wg.shape[1]
    segmented = len(rows_out) > 1 or len(rows_in) > 1
    cands = (512, 256, 128, 64, 32, 16, 8) if segmented else (1024, 512, 256, 128, 64, 32, 16, 8)
    tm = _pick(math.gcd(*rows_in, *rows_out), cands)
    tf = _pick(ff, (512, 256, 128))
    final = g_final is not None
    in_starts = _seg_starts(rows_in, tm)
    out_starts = _seg_starts(rows_out, tm)
    in_specs = [_seg_spec(in_starts, k, tm, d, ngrid=2, pipeline_mode=pl.Buffered(1)) for k in range(len(xs))]
    in_specs += [
        pl.BlockSpec((1, d), lambda i, f: (0, 0)),
        pl.BlockSpec((d, tf), lambda i, f: (0, f)),
        pl.BlockSpec((d, tf), lambda i, f: (0, f)),
        pl.BlockSpec((tf, d), lambda i, f: (f, 0)),
    ]
    args = [*xs, g.reshape(1, d), wg, wu, wd]
    if final:
        in_specs.append(pl.BlockSpec((1, d), lambda i, f: (0, 0)))
        args.append(g_final.reshape(1, d))
    scratch = [pltpu.VMEM((tm, d), BF16)]
    if len(rows_out) > 1:
        scratch.append(pltpu.VMEM((tm, d), F32))
    outs = pl.pallas_call(
        functools.partial(_ffn_body, in_starts=in_starts, out_starts=out_starts, final=final),
        grid=(m // tm, ff // tf),
        in_specs=in_specs,
        out_specs=[_seg_spec(out_starts, k, tm, d, ngrid=2) for k in range(len(rows_out))],
        out_shape=[jax.ShapeDtypeStruct((r, d), F32) for r in rows_out],
        scratch_shapes=scratch,
        compiler_params=_cparams(("parallel", "arbitrary")),
    )(*args)
    return outs if out_rows is not None else outs[0]


def _proj_body(x_ref, g_ref, w_ref, wgate_ref, cs_ref, kp_in, vp_in, ks_in, vs_in,
               p_ref, gate_ref, gatet_ref, kp_ref, vp_ref, ks_ref, vs_ref, hn_ref,
               *, j_k, j_v, npt, nhead, dh, page, t_new):
    del kp_in, vp_in, ks_in, vs_in
    i = pl.program_id(0)
    j = pl.program_id(1)
    tm = x_ref.shape[0]

    @pl.when(j == 0)
    def _():
        hn = _rms(x_ref[...], g_ref[...]).astype(BF16)
        hn_ref[...] = hn
        gate = _dot(hn, wgate_ref[...])
        gate_ref[...] = gate
        gatet_ref[...] = gate.T[:SUBLANE, :]

    acc = _dot(hn_ref[...], w_ref[...])
    p_ref[...] = (acc * cs_ref[...]).astype(BF16)

    def put(pages_ref, rows_ref):
        @pl.when(i < npt)
        def _():
            for pg in range(tm // page):
                for h in range(nhead):
                    pages_ref[0, pg, h] = acc[pg * page:(pg + 1) * page, h * dh:(h + 1) * dh]

        @pl.when(i >= npt)
        def _():
            for h in range(nhead):
                rows_ref[0, :, h] = acc[:, h * dh:(h + 1) * dh].reshape(tm // t_new, t_new, dh)

    pl.when(j == j_k)(functools.partial(put, kp_ref, ks_ref))
    pl.when(j == j_v)(functools.partial(put, vp_ref, vs_ref))


def _proj(x, g, w_main, w_gate, col_scale, caches, layer, *, mp, nhead, dh, page, t_new):
    m, d = x.shape
    n = w_main.shape[1]
    w = nhead * dh
    tn = w
    tm = _pick(math.gcd(mp, m - mp), (1024, 512, 256, 128))
    assert n % tn == 0 and tm % page == 0 and tm % t_new == 0
    j_k = (n - 2 * w) // tn
    npt = mp // tm
    nst = (m - mp) // tm
    once = dict(pipeline_mode=pl.Buffered(1))
    alias = pl.BlockSpec(memory_space=pl.ANY)
    pages_spec = pl.BlockSpec((1, tm // page, nhead, page, dh),
                              lambda i, j: (layer, jnp.minimum(i, npt - 1), 0, 0, 0), **once)
    rows_spec = pl.BlockSpec((1, tm // t_new, nhead, t_new, dh),
                             lambda i, j: (layer, jnp.clip(i - npt, 0, nst - 1), 0, 0, 0), **once)
    outs = pl.pallas_call(
        functools.partial(_proj_body, j_k=j_k, j_v=j_k + 1, npt=npt, nhead=nhead, dh=dh, page=page, t_new=t_new),
        grid=(m // tm, n // tn),
        in_specs=[
            pl.BlockSpec((tm, d), lambda i, j: (i, 0), **once),
            pl.BlockSpec((1, d), lambda i, j: (0, 0)),
            pl.BlockSpec((d, tn), lambda i, j: (0, j)),
            pl.BlockSpec((d, LANE), lambda i, j: (0, 0)),
            pl.BlockSpec((1, tn), lambda i, j: (0, j)),
            alias, alias, alias, alias,
        ],
        out_specs=[
            pl.BlockSpec((tm, tn), lambda i, j: (i, j)),
            pl.BlockSpec((tm, LANE), lambda i, j: (i, 0)),
            pl.BlockSpec((SUBLANE, tm), lambda i, j: (0, i)),
            pages_spec, pages_spec, rows_spec, rows_spec,
        ],
        out_shape=[jax.ShapeDtypeStruct((m, n), BF16), jax.ShapeDtypeStruct((m, LANE), F32),
                   jax.ShapeDtypeStruct((SUBLANE, m), F32)]
        + [jax.ShapeDtypeStruct(c.shape, c.dtype) for c in caches],
        input_output_aliases={5: 3, 6: 4, 7: 5, 8: 6},
        scratch_shapes=[pltpu.VMEM((tm, d), BF16)],
        compiler_params=_cparams(("parallel", "arbitrary")),
    )(x, g.reshape(1, d), w_main, w_gate, col_scale, *caches)
    return outs[0], outs[1], outs[2], tuple(outs[3:])


def _log_sigmoid(x):
    return jnp.minimum(x, 0.0) - jnp.log(1.0 + jnp.exp(-jnp.abs(x)))


def _mlstm_body(*refs, nh, dh, lq, lk, has_init):
    if has_init:
        (q_ref, k_ref, v_ref, o_ref, gc_ref, gr_ref, bc_ref, br_ref, gh_ref,
         c0_ref, n0_ref, m0_ref, c_alias, h_ref, cn_ref, nn_ref, mn_ref, c_sc, n_sc, m_sc) = refs
    else:
        (q_ref, k_ref, v_ref, o_ref, gc_ref, gr_ref, bc_ref, br_ref, gh_ref,
         c_alias, h_ref, cn_ref, nn_ref, mn_ref, c_sc, n_sc, m_sc) = refs
    del c_alias
    c = pl.program_id(1)
    last = pl.num_programs(1) - 1

    @pl.when(c == 0)
    def _():
        if has_init:
            c_sc[...] = c0_ref[0, 0].astype(F32)
            n_sc[...] = n0_ref[0, 0].astype(F32)
            m_sc[...] = m0_ref[0, 0].astype(F32)
        else:
            c_sc[...] = jnp.zeros_like(c_sc)
            n_sc[...] = jnp.zeros_like(n_sc)
            m_sc[...] = jnp.zeros_like(m_sc)

    gcol = gc_ref[...] + bc_ref[...]
    grow = gr_ref[...] + br_ref[...]
    lf_col = _log_sigmoid(gcol)
    lf_row = _log_sigmoid(grow)
    if lq >= LANE:
        ri = lax.broadcasted_iota(jnp.int32, (lq, lq), 0)
        ci = lax.broadcasted_iota(jnp.int32, (lq, lq), 1)
        tril = jnp.where(ci <= ri, 1.0, 0.0).astype(F32)
        triu = jnp.where(ri <= ci, 1.0, 0.0).astype(F32)
        b_col = jnp.dot(tril, lf_col, precision=lax.Precision.HIGHEST, preferred_element_type=F32)
        b_row = jnp.dot(lf_row, triu, precision=lax.Precision.HIGHEST, preferred_element_type=F32)
    else:
        ri = lax.broadcasted_iota(jnp.int32, (lq, LANE), 0)
        ci = lax.broadcasted_iota(jnp.int32, (SUBLANE, lk), 1)
        b_col = jnp.zeros((lq, LANE), F32)
        b_row = jnp.zeros((SUBLANE, lk), F32)
        for s in range(lq):
            b_col = b_col + jnp.where(ri >= s, lf_col[s:s + 1, :], 0.0)
            b_row = b_row + jnp.where(ci >= s, lf_row[:, s:s + 1], 0.0)

    qi = lax.broadcasted_iota(jnp.int32, (lq, lk), 0)
    ki = lax.broadcasted_iota(jnp.int32, (lq, lk), 1)
    causal = ki <= qi
    kvalid = lax.broadcasted_iota(jnp.int32, (lk, 1), 0) < lq
    q_all = q_ref[...]
    k_all = k_ref[...]
    v_all = v_ref[...]
    o_all = o_ref[...]
    gh_all = gh_ref[...]
    m_prev_all = m_sc[...]

    def pad_keys(a):
        if lk == lq:
            return a
        return jnp.concatenate([a, jnp.zeros((lk - lq, a.shape[1]), a.dtype)], axis=0)

    for h in range(nh):
        sl = slice(h * dh, (h + 1) * dh)
        qh = q_all[:, sl].astype(F32)
        kh = pad_keys(k_all[:, sl]).astype(F32)
        vh = pad_keys(v_all[:, sl]).astype(F32)
        bq = b_col[:, nh + h:nh + h + 1]
        ig_k = pad_keys(gcol[:, h:h + 1])
        bk = pad_keys(bq)
        r_row = grow[h:h + 1, :] - b_row[nh + h:nh + h + 1, :]
        m_prev = m_prev_all[:, h:h + 1]
        log_d = jnp.where(causal, bq + r_row, -jnp.inf)
        log_p = bq + m_prev
        m_t = jnp.maximum(log_p, jnp.max(log_d, axis=-1, keepdims=True))
        w_intra = jnp.exp(log_d - m_t)
        w_prev = jnp.exp(log_p - m_t)
        qb = qh.astype(BF16)
        kb = kh.astype(BF16)
        s = _dot_nt(qb, kb) * w_intra
        cmat = c_sc[h]
        nrow = n_sc[h:h + 1, :]
        num = _dot(s.astype(BF16), vh.astype(BF16)) + w_prev * _dot_nt(qb, cmat.astype(BF16))
        den = jnp.sum(s, axis=-1, keepdims=True) + w_prev * jnp.sum(qh * nrow, axis=-1, keepdims=True)
        hh = num / jnp.maximum(jnp.abs(den), jnp.exp(-m_t))
        b_last = bq[lq - 1:lq, :]
        m_last = m_t[lq - 1:lq, :]
        w_last = jnp.where(kvalid, jnp.exp(b_last - bk + ig_k - m_last), 0.0)
        decay = w_prev[lq - 1:lq, :]
        c_new = decay * cmat + _dot_tn((vh * w_last).astype(BF16), kb)
        n_new = decay * nrow + jnp.sum(kh * w_last, axis=0, keepdims=True)
        c_sc[h] = c_new
        n_sc[h:h + 1, :] = n_new
        m_sc[:, h:h + 1] = m_last
        hm = jax.nn.sigmoid(o_all[:, sl].astype(F32)) * hh
        h_ref[:, sl] = _rms(hm, gh_all[:, sl])

    @pl.when(c == last)
    def _():
        cn_ref[0, 0] = c_sc[...]
        nn_ref[0] = n_sc[...]
        mn_ref[0] = m_sc[...]


def _mlstm(p, gates, gates_t, bias_c, bias_r, g_head, c_stack, layer, *, row_off, bsz, seq, chunk, col0, nh, dh,
           init=None):
    w = nh * dh
    lq = chunk
    lk = max(chunk, LANE)
    nc = seq // chunk
    rb0 = row_off // lq
    has_init = init is not None

    def rowblk(b, c):
        return rb0 + b * nc + c

    in_specs = [
        pl.BlockSpec((lq, w), lambda b, c: (rowblk(b, c), col0)),
        pl.BlockSpec((lq, w), lambda b, c: (rowblk(b, c), col0 + 1)),
        pl.BlockSpec((lq, w), lambda b, c: (rowblk(b, c), col0 + 2)),
        pl.BlockSpec((lq, w), lambda b, c: (rowblk(b, c), col0 + 3)),
        pl.BlockSpec((lq, LANE), lambda b, c: (rowblk(b, c), 0)),
        pl.BlockSpec((SUBLANE, lk), lambda b, c: (0, rowblk(b, c))),
        pl.BlockSpec((1, LANE), lambda b, c: (0, 0)),
        pl.BlockSpec((SUBLANE, 1), lambda b, c: (0, 0)),
        pl.BlockSpec((1, w), lambda b, c: (0, 0)),
    ]
    args = [p, p, p, p, gates, gates_t, bias_c, bias_r, g_head.reshape(1, w)]
    if has_init:
        c0, n0, m0 = init
        in_specs += [
            pl.BlockSpec((1, 1, nh, dh, dh), lambda b, c: (layer, b, 0, 0, 0)),
            pl.BlockSpec((1, 1, nh, dh), lambda b, c: (layer, b, 0, 0)),
            pl.BlockSpec((1, 1, 1, nh), lambda b, c: (layer, b, 0, 0)),
        ]
        args += [c0, n0, m0.reshape(m0.shape[0], bsz, 1, nh)]
    in_specs.append(pl.BlockSpec(memory_space=pl.ANY))
    args.append(c_stack)
    out_specs = [
        pl.BlockSpec((lq, w), lambda b, c: (b * nc + c, 0)),
        pl.BlockSpec((1, 1, nh, dh, dh), lambda b, c: (layer, b, 0, 0, 0)),
        pl.BlockSpec((1, nh, dh), lambda b, c: (b, 0, 0)),
        pl.BlockSpec((1, 1, nh), lambda b, c: (b, 0, 0)),
    ]
    out_shape = [
        jax.ShapeDtypeStruct((bsz * seq, w), F32),
        jax.ShapeDtypeStruct(c_stack.shape, F32),
        jax.ShapeDtypeStruct((bsz, nh, dh), F32),
        jax.ShapeDtypeStruct((bsz, 1, nh), F32),
    ]
    h, cn, nn, mn = pl.pallas_call(
        functools.partial(_mlstm_body, nh=nh, dh=dh, lq=lq, lk=lk, has_init=has_init),
        grid=(bsz, nc),
        in_specs=in_specs,
        out_specs=out_specs,
        out_shape=out_shape,
        input_output_aliases={len(args) - 1: 1},
        scratch_shapes=[pltpu.VMEM((nh, dh, dh), F32), pltpu.VMEM((nh, dh), F32), pltpu.VMEM((1, nh), F32)],
        compiler_params=_cparams(("parallel", "arbitrary")),
    )(*args)
    return h, (cn, nn, mn.reshape(bsz, nh))


def _bucket_np(dist):
    n = np.maximum(dist, 0)
    max_exact = REL_BUCKETS // 2
    nf = np.maximum(n, 1).astype(np.float32)
    large = max_exact + (np.log(nf / np.float32(max_exact)) / np.float32(math.log(REL_MAX_DIST / max_exact))
                         * np.float32(REL_BUCKETS - max_exact)).astype(np.int32)
    large = np.minimum(large, REL_BUCKETS - 1)
    return np.where(dist < 0, -1, np.where(n < max_exact, n, large)).astype(np.int32)


def _bias_body(tbl_ref, map_ref, o_ref, *, ntile):
    h = pl.program_id(0)
    for t in range(ntile):
        bm = map_ref[t]
        acc = jnp.full(bm.shape, NEG, F32)
        for b in range(REL_BUCKETS):
            acc = jnp.where(bm == b, tbl_ref[h, b], acc)
        o_ref[0, t] = acc


def _bias_tiles(rel_t, maps):
    nhead = rel_t.shape[0]
    ntile, r, c = maps.shape
    return pl.pallas_call(
        functools.partial(_bias_body, ntile=ntile),
        grid=(nhead,),
        in_specs=[
            pl.BlockSpec(memory_space=pltpu.SMEM),
            pl.BlockSpec((ntile, r, c), lambda h: (0, 0, 0)),
        ],
        out_specs=pl.BlockSpec((1, ntile, r, c), lambda h: (h, 0, 0, 0)),
        out_shape=jax.ShapeDtypeStruct((nhead, ntile, r, c), F32),
        compiler_params=_cparams(("arbitrary",)),
    )(rel_t, jnp.asarray(maps))


def _prompt_bias_maps():
    k = np.arange(MOBA_BLOCK)[:, None]
    q = np.arange(MOBA_BLOCK)[None, :]
    return np.stack([_bucket_np(q - k), _bucket_np(MOBA_BLOCK + q - k)])


def _sample_bias_maps(t_new):
    r = np.arange(t_new)[:, None]
    c = np.arange(MOBA_BLOCK)[None, :]
    own = np.where(c < t_new, r - c, -1)
    return np.stack([_bucket_np(MOBA_BLOCK + r - c), _bucket_np(own), _bucket_np(2 * MOBA_BLOCK + r - c + t_new)])


def _topk_select(scores):
    n = len(scores)
    sel = []
    for a in range(n):
        cnt = jnp.zeros(scores[a].shape, F32)
        for b in range(n):
            if b == a:
                continue
            ahead = (scores[b] >= scores[a]) if b < a else (scores[b] > scores[a])
            cnt = cnt + jnp.where(ahead, 1.0, 0.0)
        keep = cnt < MOBA_TOPK
        sel.append(keep)
    return sel


def _moba_p_body(tbl_ref, q_ref, k_ref, v_ref, bias_ref, o_ref, km_sc, vt_sc, lg_sc, *, nblk, dh, hps):
    hg = pl.program_id(1)
    j = pl.program_id(2)
    blk = MOBA_BLOCK

    @pl.when(j == 0)
    def _():
        km_sc[...] = jnp.zeros_like(km_sc)
        for hh in range(hps):
            sl = slice(hh * dh, (hh + 1) * dh)
            for b in range(nblk):
                kblk = k_ref[b * blk:(b + 1) * blk, sl].astype(F32)
                km_sc[hh, b:b + 1, :] = jnp.mean(kblk, axis=0, keepdims=True)
            vt_sc[hh] = v_ref[:, sl].astype(F32).T.astype(BF16)

    def select(hh):
        qs = q_ref[:, hh * dh:(hh + 1) * dh]
        st = _dot_nt(km_sc[hh].astype(BF16), qs)
        rowi = lax.broadcasted_iota(jnp.int32, st.shape, 0)
        st = jnp.where(rowi < j, st, -jnp.inf)
        cnt = jnp.zeros(st.shape, F32)
        for b in range(nblk):
            sb = st[b:b + 1, :]
            cnt = cnt + jnp.where(rowi > b, jnp.where(sb >= st, 1.0, 0.0), jnp.where(sb > st, 1.0, 0.0))
        return qs, jnp.where(cnt < MOBA_TOPK, jnp.where(rowi < j, 1.0, 0.0), 0.0)

    picked = [select(hh) for hh in range(hps)]

    def run(jj):
        for hh in range(hps):
            sl = slice(hh * dh, (hh + 1) * dh)
            qs, sel_t = picked[hh]
            far = tbl_ref[hg * hps + hh, REL_BUCKETS - 1]
            m = None
            for b in range(jj + 1):
                lg = _dot_nt(k_ref[b * blk:(b + 1) * blk, sl], qs)
                if b == jj:
                    lg = lg + bias_ref[hh, 0]
                elif b == jj - 1:
                    lg = lg + bias_ref[hh, 1]
                else:
                    lg = lg + far
                if b < jj:
                    lg = jnp.where(sel_t[b:b + 1, :] > 0.5, lg, NEG)
                lg_sc[hh, b] = lg
                tmax = jnp.max(lg, axis=0, keepdims=True)
                m = tmax if m is None else jnp.maximum(m, tmax)
            l = jnp.zeros_like(m)
            acc = jnp.zeros((dh, blk), F32)
            for b in range(jj + 1):
                pr = jnp.exp(lg_sc[hh, b] - m)
                l = l + jnp.sum(pr, axis=0, keepdims=True)
                acc = acc + _dot(vt_sc[hh, :, b * blk:(b + 1) * blk], pr.astype(BF16))
            o_ref[:, sl] = (acc / l).T

    for jj in range(nblk):
        pl.when(j == jj)(functools.partial(run, jj))


def _moba_prompt(rel_t, p, bias, *, bsz, seq, colq, colk, colv, nhead, dh):
    assert dh == LANE and seq % MOBA_BLOCK == 0 and MOBA_BLOCK >= REL_MAX_DIST
    nblk = seq // MOBA_BLOCK
    assert nblk <= SUBLANE
    ntile = bias.shape[1]
    hps = 2
    assert nhead % hps == 0 and colq % hps == 0 and colk % hps == 0 and colv % hps == 0
    cq, ck, cv = colq // hps, colk // hps, colv // hps
    return pl.pallas_call(
        functools.partial(_moba_p_body, nblk=nblk, dh=dh, hps=hps),
        grid=(bsz, nhead // hps, nblk),
        in_specs=[
            pl.BlockSpec(memory_space=pltpu.SMEM),
            pl.BlockSpec((MOBA_BLOCK, hps * dh), lambda b, h, j: (b * nblk + j, cq + h)),
            pl.BlockSpec((seq, hps * dh), lambda b, h, j: (b, ck + h)),
            pl.BlockSpec((seq, hps * dh), lambda b, h, j: (b, cv + h)),
            pl.BlockSpec((hps, ntile, MOBA_BLOCK, MOBA_BLOCK), lambda b, h, j: (h, 0, 0, 0)),
        ],
        out_specs=pl.BlockSpec((MOBA_BLOCK, hps * dh), lambda b, h, j: (b * nblk + j, h)),
        out_shape=jax.ShapeDtypeStruct((bsz * seq, nhead * dh), F32),
        scratch_shapes=[
            pltpu.VMEM((hps, 2 * SUBLANE, dh), F32),
            pltpu.VMEM((hps, dh, seq), BF16),
            pltpu.VMEM((hps, nblk, MOBA_BLOCK, MOBA_BLOCK), F32),
        ],
        compiler_params=_cparams(("parallel", "parallel", "arbitrary")),
    )(rel_t, p, p, p, bias)


def _moba_s_body(pt_ref, q_ref, kn_ref, vn_ref, bias_ref, *rest, npage, nhead, dh, t_new, page, bsz):
    kp = rest[:npage]
    vp = rest[npage:2 * npage]
    o_ref, p_sc, l_sc, acc_sc = rest[2 * npage:]
    s = pl.program_id(0)
    ppb = MOBA_BLOCK // page
    nblk = npage // ppb

    def heads(x):
        return jnp.stack([x[:, h * dh:(h + 1) * dh] for h in range(nhead)], axis=0)

    def pad_rows(x):
        return jnp.concatenate([x, jnp.zeros((page - t_new, x.shape[1]), x.dtype)], axis=0)

    @pl.when(s >= 1)
    def _():
        slot = (s + 1) % 2
        acc = acc_sc[slot]
        for pg in range(npage):
            acc = acc + jnp.einsum('htk,hkd->htd', p_sc[slot, pg].astype(BF16), vp[pg][0, 0].astype(BF16),
                                   preferred_element_type=F32)
        out = acc / l_sc[slot]
        for h in range(nhead):
            o_ref[:, h * dh:(h + 1) * dh] = out[h]

    @pl.when(s < bsz)
    def _():
        slot = s % 2
        q3 = heads(q_ref[...]).astype(BF16)
        lgs =[jnp.einsum('htd,hkd->htk', q3, kp[pg][0, 0].astype(BF16), preferred_element_type=F32)
               for pg in range(npage)]
        scores = []
        for b in range(nblk):
            tot = lgs[b * ppb]
            for r in range(1, ppb):
                tot = tot + lgs[b * ppb + r]
            scores.append(jnp.sum(tot, axis=-1, keepdims=True))
        sel = _topk_select(scores)
        far = bias_ref[:, 2]
        prev = bias_ref[:, 0]
        ml = []
        for pg in range(npage):
            b = pg // ppb
            r = pg % ppb
            bias = prev[:, :, r * page:(r + 1) * page] if b == nblk - 1 else far[:, :, :page]
            ml.append(jnp.where(sel[b], lgs[pg] + bias, NEG))
        kn3 = heads(pad_rows(kn_ref[...])).astype(BF16)
        vn3 = heads(pad_rows(vn_ref[...])).astype(BF16)
        lo = jnp.einsum('htd,hkd->htk', q3, kn3, preferred_element_type=F32) + bias_ref[:, 1][:, :, :page]
        m = jnp.max(lo, axis=-1, keepdims=True)
        for x in ml:
            m = jnp.maximum(m, jnp.max(x, axis=-1, keepdims=True))
        eo = jnp.exp(lo - m)
        l = jnp.sum(eo, axis=-1, keepdims=True)
        for pg in range(npage):
            e = jnp.exp(ml[pg] - m)
            l = l + jnp.sum(e, axis=-1, keepdims=True)
            p_sc[slot, pg] = e
        l_sc[slot] = l
        acc_sc[slot] = jnp.einsum('htk,hkd->htd', eo.astype(BF16), vn3, preferred_element_type=F32)


def _moba_sample(p, bias, kpool, vpool, layer, page_table, *, row_off, bsz, t_new, colq, colk, colv, nhead, dh):
    npage = page_table.shape[1]
    page = kpool.shape[3]
    assert dh == LANE and page == LANE and MOBA_BLOCK % page == 0
    assert (npage * page) % MOBA_BLOCK == 0 and npage * page >= MOBA_BLOCK and t_new <= page
    w = nhead * dh
    rb0 = row_off // t_new
    ntile = bias.shape[1]
    pt = page_table.reshape(-1).astype(jnp.int32)
    assert colq % nhead == 0 and colk % nhead == 0 and colv % nhead == 0
    colq, colk, colv = colq // nhead, colk // nhead, colv // nhead

    def cur(s):
        return jnp.minimum(s, bsz - 1)

    def prev(s):
        return jnp.maximum(s - 1, 0)

    def kmap(pg):
        return lambda s, pt_ref: (layer, pt_ref[cur(s) * npage + pg], 0, 0, 0)

    def vmap_(pg):
        return lambda s, pt_ref: (layer, pt_ref[prev(s) * npage + pg], 0, 0, 0)

    in_specs = [
        pl.BlockSpec((t_new, w), lambda s, pt_ref: (rb0 + cur(s), colq)),
        pl.BlockSpec((t_new, w), lambda s, pt_ref: (rb0 + cur(s), colk)),
        pl.BlockSpec((t_new, w), lambda s, pt_ref: (rb0 + cur(s), colv)),
        pl.BlockSpec((nhead, ntile, t_new, MOBA_BLOCK), lambda s, pt_ref: (0, 0, 0, 0)),
    ]
    in_specs += [pl.BlockSpec((1, 1, nhead, page, dh), kmap(pg)) for pg in range(npage)]
    in_specs += [pl.BlockSpec((1, 1, nhead, page, dh), vmap_(pg)) for pg in range(npage)]
    grid_spec = pltpu.PrefetchScalarGridSpec(
        num_scalar_prefetch=1,
        grid=(bsz + 1,),
        in_specs=in_specs,
        out_specs=pl.BlockSpec((t_new, w), lambda s, pt_ref: (prev(s), 0)),
        scratch_shapes=[
            pltpu.VMEM((2, npage, nhead, t_new, page), F32),
            pltpu.VMEM((2, nhead, t_new, 1), F32),
            pltpu.VMEM((2, nhead, t_new, dh), F32),
        ],
    )
    return pl.pallas_call(
        functools.partial(_moba_s_body, npage=npage, nhead=nhead, dh=dh, t_new=t_new, page=page, bsz=bsz),
        grid_spec=grid_spec,
        out_shape=jax.ShapeDtypeStruct((bsz * t_new, w), F32),
        compiler_params=_cparams(("arbitrary",)),
    )(pt, p, p, p, bias, *([kpool] * npage), *([vpool] * npage))


def _mix_body(*refs, starts):
    ns = len(starts) - 1
    hm_refs, hb_refs = refs[:ns], refs[ns:2 * ns]
    ga_ref, gb_ref, x_ref, wa_ref, wb_ref, wo_ref, o_ref = refs[2 * ns:]
    i = pl.program_id(0)
    a = _dot(_seg_load(hm_refs, starts, i).astype(BF16), wa_ref[...])
    b = _dot(_seg_load(hb_refs, starts, i).astype(BF16), wb_ref[...])
    merged = jax.nn.sigmoid(ga_ref[...].astype(F32)) * a + jax.nn.sigmoid(gb_ref[...].astype(F32)) * b
    o_ref[...] = x_ref[...] + _dot(merged.astype(BF16), wo_ref[...])


def _mix(hms, hbs, p, x, wa, wb, wo):
    m, d = x.shape
    wm = hms[0].shape[1]
    wbw = hbs[0].shape[1]
    rows = [h.shape[0] for h in hms]
    assert rows == [h.shape[0] for h in hbs] and sum(rows) == m
    tm = _pick(math.gcd(*rows), (256, 128, 64, 32, 16, 8))
    starts = _seg_starts(rows, tm)
    const = dict(pipeline_mode=pl.Buffered(1))
    return pl.pallas_call(
        functools.partial(_mix_body, starts=starts),
        grid=(m // tm,),
        in_specs=[_seg_spec(starts, k, tm, wm) for k in range(len(rows))]
        + [_seg_spec(starts, k, tm, wbw) for k in range(len(rows))]
        + [
            pl.BlockSpec((tm, d), lambda i: (i, 0)),
            pl.BlockSpec((tm, d), lambda i: (i, 1)),
            pl.BlockSpec((tm, d), lambda i: (i, 0)),
            pl.BlockSpec((wm, d), lambda i: (0, 0), **const),
            pl.BlockSpec((wbw, d), lambda i: (0, 0), **const),
            pl.BlockSpec((d, d), lambda i: (0, 0), **const),
        ],
        out_specs=pl.BlockSpec((tm, d), lambda i: (i, 0)),
        out_shape=jax.ShapeDtypeStruct((m, d), F32),
        compiler_params=_cparams(("parallel",)),
    )(*hms, *hbs, p, p, x, wa, wb, wo)


def kernel(x_prompt, x_sample, cache_k, cache_v, state_C, state_n, state_m, page_table, g_ff1, w_ff1_gate,
           w_ff1_up, w_ff1_down, g_mix, w_in, b_ig, b_fg, g_head, w_a, w_b, w_out, g_ff2, w_ff2_gate,
           w_ff2_up, w_ff2_down, rel_bias_table, g_final):
    bp, seq, d = x_prompt.shape
    bd, t_new, _ = x_sample.shape
    depth = g_ff1.shape[0]
    nh_m = b_ig.shape[1]
    w_m = w_a.shape[1]
    dh_m = w_m // nh_m
    nh_b = rel_bias_table.shape[1]
    w_bw = w_b.shape[1]
    dh_b = w_bw // nh_b
    page = cache_k.shape[3]
    mp = bp * seq
    sdt = state_C.dtype
    assert 2 * nh_m <= SUBLANE and (2 * d) % w_m == 0 and (2 * d + 4 * w_m) % LANE == 0
    assert mp % t_new == 0 and seq % MLSTM_PROMPT_CHUNK == 0 and seq % page == 0

    o_gate = 4 * w_m
    o_moba = o_gate + 2 * nh_m
    o_ga = o_moba + 3 * w_bw
    col_m = (2 * d) // w_m
    col_b = (2 * d + 4 * w_m) // dh_b
    colq_b, colk_b, colv_b = col_b, col_b + nh_b, col_b + 2 * nh_b

    n_slab = 2 * d + 4 * w_m + 3 * w_bw
    col_scale = np.ones((1, n_slab), np.float32)
    col_scale[:, 2 * d + w_m:2 * d + 2 * w_m] = dh_m ** -0.5
    col_scale[:, 2 * d + 4 * w_m:2 * d + 4 * w_m + w_bw] = dh_b ** -0.5
    col_scale = jnp.asarray(col_scale)

    rel_t = rel_bias_table.T.astype(F32)
    bias_p = _bias_tiles(rel_t, _prompt_bias_maps())
    bias_s = _bias_tiles(rel_t, _sample_bias_maps(t_new))

    ms_rows = bd * t_new
    xs = [x_prompt.reshape(mp, d), x_sample.reshape(ms_rows, d)]
    caches = (jnp.zeros((depth, mp // page, nh_b, page, dh_b), F32), jnp.zeros((depth, mp // page, nh_b, page, dh_b), F32),
              jnp.zeros((depth, bd, nh_b, t_new, dh_b), F32), jnp.zeros((depth, bd, nh_b, t_new, dh_b), F32))
    c_prompt = jnp.zeros((depth, bp, nh_m, dh_m, dh_m), F32)
    c_sample = jnp.zeros((depth, bd, nh_m, dh_m, dh_m), F32)
    np_l, mp_l, ns_l, ms_l = [], [], [], []
    for l in range(depth):
        wi = w_in[l]
        w_main = jnp.concatenate([wi[:, o_ga:], wi[:, :o_gate], wi[:, o_moba:o_ga]], axis=1).astype(BF16)
        w_gate = jnp.pad(wi[:, o_gate:o_moba], ((0, 0), (0, LANE - 2 * nh_m))).astype(BF16)
        gate_bias = jnp.concatenate([b_ig[l], b_fg[l]]).astype(F32)
        bias_c = jnp.pad(gate_bias, (0, LANE - 2 * nh_m)).reshape(1, LANE)
        bias_r = jnp.pad(gate_bias, (0, SUBLANE - 2 * nh_m)).reshape(SUBLANE, 1)

        x = _ffn(xs, g_ff1[l], _cast_layer(w_ff1_gate, l), _cast_layer(w_ff1_up, l), _cast_layer(w_ff1_down, l))
        p, gates, gt_p, caches = _proj(x, g_mix[l], w_main, w_gate, col_scale, caches, l, mp=mp, nhead=nh_b,
                                       dh=dh_b, page=page, t_new=t_new)
        p_s = p[mp:].astype(F32)

        gt_s = gt_p[:, mp:].reshape(SUBLANE, bd, t_new)
        gt_s = jnp.pad(gt_s, ((0, 0), (0, 0), (0, LANE - t_new))).reshape(SUBLANE, bd * LANE)

        mk = dict(col0=col_m, nh=nh_m, dh=dh_m)
        hm_p, (c_prompt, npp, mpp) = _mlstm(p, gates, gt_p, bias_c, bias_r, g_head[l], c_prompt, l, row_off=0,
                                            bsz=bp, seq=seq, chunk=MLSTM_PROMPT_CHUNK, **mk)
        init = (state_C, state_n, state_m)
        hm_s, (c_sample, ns, ms) = _mlstm(p_s, gates[mp:], gt_s, bias_c, bias_r, g_head[l], c_sample, l, row_off=0,
                                          bsz=bd, seq=t_new, chunk=t_new, init=init, **mk)

        bk = dict(colq=colq_b, colk=colk_b, colv=colv_b, nhead=nh_b, dh=dh_b)
        hb_p = _moba_prompt(rel_t, p, bias_p, bsz=bp, seq=seq, **bk)
        hb_s = _moba_sample(p_s, bias_s, cache_k, cache_v, l, page_table, row_off=0, bsz=bd, t_new=t_new, **bk)

        x = _mix([hm_p, hm_s], [hb_p, hb_s], p, x, _cast_layer(w_a, l), _cast_layer(w_b, l), _cast_layer(w_out, l))
        ffn2 = (g_ff2[l], _cast_layer(w_ff2_gate, l), _cast_layer(w_ff2_up, l), _cast_layer(w_ff2_down, l))
        if l == depth - 1:
            y_prompt, y_sample = _ffn([x], *ffn2, g_final=g_final, out_rows=(mp, ms_rows))
        else:
            xs = [_ffn([x], *ffn2)]

        np_l.append(npp.astype(sdt))
        mp_l.append(mpp.astype(sdt))
        ns_l.append(ns.astype(sdt))
        ms_l.append(ms.astype(sdt))

    k_pages, v_pages, k_rows, v_rows = caches
    page_shape = (depth, bp, seq // page, nh_b, page, dh_b)
    return (y_prompt.reshape(bp, seq, d), y_sample.reshape(bd, t_new, d),
            k_pages.reshape(page_shape), v_pages.reshape(page_shape), k_rows, v_rows,
            c_prompt.astype(sdt), jnp.stack(np_l), jnp.stack(mp_l), c_sample.astype(sdt), jnp.stack(ns_l), jnp.stack(ms_l))
```

```python
import functools
import math

import numpy as np
import jax
import jax.numpy as jnp
from jax import lax
from jax.experimental import pallas as pl
from jax.experimental.pallas import tpu as pltpu

F32 = jnp.float32
BF16 = jnp.bfloat16

NORM_EPS = 1e-6
MOBA_BLOCK = 256
MOBA_TOPK = 3
MOBA_QBLOCK = 128
REL_BUCKETS = 32
REL_MAX_DIST = 128
MLSTM_PROMPT_CHUNK = 256
LANE = 128
SUBLANE = 8
NEG = -1e30
VMEM_LIMIT = 56 * 1024 * 1024


def _pick(n, cands):
    for c in cands:
        if n % c == 0:
            return c
    return n


def _cparams(sem):
    return pltpu.CompilerParams(dimension_semantics=sem, vmem_limit_bytes=VMEM_LIMIT)


def _rms(x, g):
    return x * lax.rsqrt(jnp.mean(x * x, axis=-1, keepdims=True) + NORM_EPS) * g


def _dot(a, b):
    return jnp.dot(a, b, preferred_element_type=F32)


def _dot_nt(a, b):
    return lax.dot_general(a, b, (((1,), (1,)), ((), ())), preferred_element_type=F32)


def _dot_tn(a, b):
    return lax.dot_general(a, b, (((0,), (0,)), ((), ())), preferred_element_type=F32)


CAST_BLOCK_BYTES = 6 * 1024 * 1024


def _cast_body(w_ref, o_ref):
    o_ref[...] = w_ref[0].astype(o_ref.dtype)


def _cast_layer(w, layer):
    _, rows, cols = w.shape
    tr = rows
    while tr % 2 == 0 and tr // 2 >= 2 * SUBLANE and tr * cols * 4 > CAST_BLOCK_BYTES:
        tr //= 2
    return pl.pallas_call(
        _cast_body,
        grid=(rows // tr,),
        in_specs=[pl.BlockSpec((1, tr, cols), lambda i: (layer, i, 0))],
        out_specs=pl.BlockSpec((tr, cols), lambda i: (i, 0)),
        out_shape=jax.ShapeDtypeStruct((rows, cols), BF16),
        compiler_params=_cparams(("parallel",)),
    )(w)


def _seg_starts(rows, tm):
    starts = [0]
    for r in rows:
        assert r % tm == 0
        starts.append(starts[-1] + r // tm)
    return starts


def _seg_spec(starts, k, tm, width, col=0, ngrid=1, **kw):
    lo, n = starts[k], starts[k + 1] - starts[k]
    if ngrid == 1:
        return pl.BlockSpec((tm, width), lambda i: (jnp.clip(i - lo, 0, n - 1), col), **kw)
    return pl.BlockSpec((tm, width), lambda i, j: (jnp.clip(i - lo, 0, n - 1), col), **kw)


def _seg_load(refs, starts, i):
    x = refs[0][...]
    for k in range(1, len(refs)):
        x = jnp.where(i >= starts[k], refs[k][...], x)
    return x


def _ffn_body(*refs, in_starts, out_starts, final):
    nx, no = len(in_starts) - 1, len(out_starts) - 1
    x_refs = refs[:nx]
    g_ref, wg_ref, wu_ref, wd_ref = refs[nx:nx + 4]
    rest = refs[nx + 4:]
    if final:
        gf_ref, rest = rest[0], rest[1:]
    o_refs = rest[:no]
    hn_ref = rest[no]
    acc_ref = rest[no + 1] if no > 1 else o_refs[0]
    i = pl.program_id(0)
    f = pl.program_id(1)

    @pl.when(f == 0)
    def _():
        hn_ref[...] = _rms(_seg_load(x_refs, in_starts, i), g_ref[...]).astype(BF16)
        acc_ref[...] = jnp.zeros_like(acc_ref)

    h = hn_ref[...]
    g = _dot(h, wg_ref[...])
    u = _dot(h, wu_ref[...])
    a = (g * jax.nn.sigmoid(g) * u).astype(BF16)
    acc_ref[...] += _dot(a, wd_ref[...])

    @pl.when(f == pl.num_programs(1) - 1)
    def _():
        y = _seg_load(x_refs, in_starts, i) + 0.5 * acc_ref[...]
        if final:
            y = _rms(y, gf_ref[...])
        if no == 1:
            o_refs[0][...] = y
        else:
            for k in range(no):
                def store(k=k):
                    o_refs[k][...] = y
                pl.when(jnp.logical_and(i >= out_starts[k], i < out_starts[k + 1]))(store)


def _ffn(xs, g, wg, wu, wd, g_final=None, out_rows=None):
    d = xs[0].shape[1]
    rows_in = [x.shape[0] for x in xs]
    m = sum(rows_in)
    rows_out = list(out_rows) if out_rows is not None else [m]
    ff = wg.shape[1]
    tm = _pick(math.gcd(*rows_in, *rows_out), (512, 256, 128, 64, 32, 16, 8))
    tf = _pick(ff, (512, 256, 128))
    final = g_final is not None
    in_starts = _seg_starts(rows_in, tm)
    out_starts = _seg_starts(rows_out, tm)
    in_specs = [_seg_spec(in_starts, k, tm, d, ngrid=2) for k in range(len(xs))]
    in_specs += [
        pl.BlockSpec((1, d), lambda i, f: (0, 0)),
        pl.BlockSpec((d, tf), lambda i, f: (0, f)),
        pl.BlockSpec((d, tf), lambda i, f: (0, f)),
        pl.BlockSpec((tf, d), lambda i, f: (f, 0)),
    ]
    args = [*xs, g.reshape(1, d), wg, wu, wd]
    if final:
        in_specs.append(pl.BlockSpec((1, d), lambda i, f: (0, 0)))
        args.append(g_final.reshape(1, d))
    scratch = [pltpu.VMEM((tm, d), BF16)]
    if len(rows_out) > 1:
        scratch.append(pltpu.VMEM((tm, d), F32))
    outs = pl.pallas_call(
        functools.partial(_ffn_body, in_starts=in_starts, out_starts=out_starts, final=final),
        grid=(m // tm, ff // tf),
        in_specs=in_specs,
        out_specs=[_seg_spec(out_starts, k, tm, d, ngrid=2) for k in range(len(rows_out))],
        out_shape=[jax.ShapeDtypeStruct((r, d), F32) for r in rows_out],
        scratch_shapes=scratch,
        compiler_params=_cparams(("parallel", "arbitrary")),
    )(*args)
    return outs if out_rows is not None else outs[0]


def _proj_body(x_ref, g_ref, w_ref, wgate_ref, cs_ref, kp_in, vp_in, ks_in, vs_in,
               p_ref, gate_ref, gatet_ref, kp_ref, vp_ref, ks_ref, vs_ref, hn_ref,
               *, j_k, j_v, npt, nhead, dh, page, t_new):
    del kp_in, vp_in, ks_in, vs_in
    i = pl.program_id(0)
    j = pl.program_id(1)
    tm = x_ref.shape[0]

    @pl.when(j == 0)
    def _():
        hn = _rms(x_ref[...], g_ref[...]).astype(BF16)
        hn_ref[...] = hn
        gate = _dot(hn, wgate_ref[...])
        gate_ref[...] = gate
        gatet_ref[...] = gate.T[:SUBLANE, :]

    acc = _dot(hn_ref[...], w_ref[...])
    p_ref[...] = (acc * cs_ref[...]).astype(BF16)

    def put(pages_ref, rows_ref):
        @pl.when(i < npt)
        def _():
            for pg in range(tm // page):
                for h in range(nhead):
                    pages_ref[0, pg, h] = acc[pg * page:(pg + 1) * page, h * dh:(h + 1) * dh]

        @pl.when(i >= npt)
        def _():
            for h in range(nhead):
                rows_ref[0, :, h] = acc[:, h * dh:(h + 1) * dh].reshape(tm // t_new, t_new, dh)

    pl.when(j == j_k)(functools.partial(put, kp_ref, ks_ref))
    pl.when(j == j_v)(functools.partial(put, vp_ref, vs_ref))


def _proj(x, g, w_main, w_gate, col_scale, caches, layer, *, mp, nhead, dh, page, t_new):
    m, d = x.shape
    n = w_main.shape[1]
    w = nhead * dh
    tn = w
    tm = _pick(math.gcd(mp, m - mp), (1024, 512, 256, 128))
    assert n % tn == 0 and tm % page == 0 and tm % t_new == 0
    j_k = (n - 2 * w) // tn
    npt = mp // tm
    nst = (m - mp) // tm
    once = dict(pipeline_mode=pl.Buffered(1))
    alias = pl.BlockSpec(memory_space=pl.ANY)
    pages_spec = pl.BlockSpec((1, tm // page, nhead, page, dh),
                              lambda i, j: (layer, jnp.minimum(i, npt - 1), 0, 0, 0), **once)
    rows_spec = pl.BlockSpec((1, tm // t_new, nhead, t_new, dh),
                             lambda i, j: (layer, jnp.clip(i - npt, 0, nst - 1), 0, 0, 0), **once)
    outs = pl.pallas_call(
        functools.partial(_proj_body, j_k=j_k, j_v=j_k + 1, npt=npt, nhead=nhead, dh=dh, page=page, t_new=t_new),
        grid=(m // tm, n // tn),
        in_specs=[
            pl.BlockSpec((tm, d), lambda i, j: (i, 0), **once),
            pl.BlockSpec((1, d), lambda i, j: (0, 0)),
            pl.BlockSpec((d, tn), lambda i, j: (0, j)),
            pl.BlockSpec((d, LANE), lambda i, j: (0, 0)),
            pl.BlockSpec((1, tn), lambda i, j: (0, j)),
            alias, alias, alias, alias,
        ],
        out_specs=[
            pl.BlockSpec((tm, tn), lambda i, j: (i, j)),
            pl.BlockSpec((tm, LANE), lambda i, j: (i, 0)),
            pl.BlockSpec((SUBLANE, tm), lambda i, j: (0, i)),
            pages_spec, pages_spec, rows_spec, rows_spec,
        ],
        out_shape=[jax.ShapeDtypeStruct((m, n), BF16), jax.ShapeDtypeStruct((m, LANE), F32),
                   jax.ShapeDtypeStruct((SUBLANE, m), F32)]
        + [jax.ShapeDtypeStruct(c.shape, c.dtype) for c in caches],
        input_output_aliases={5: 3, 6: 4, 7: 5, 8: 6},
        scratch_shapes=[pltpu.VMEM((tm, d), BF16)],
        compiler_params=_cparams(("parallel", "arbitrary")),
    )(x, g.reshape(1, d), w_main, w_gate, col_scale, *caches)
    return outs[0], outs[1], outs[2], tuple(outs[3:])


def _log_sigmoid(x):
    return jnp.minimum(x, 0.0) - jnp.log(1.0 + jnp.exp(-jnp.abs(x)))


def _mlstm_body(*refs, nh, dh, lq, lk, ne, has_init):
    if has_init:
        (q_ref, k_ref, v_ref, o_ref, gc_ref, gr_ref, bc_ref, br_ref, gh_ref,
         c0_ref, n0_ref, m0_ref, c_alias, h_ref, cn_ref, nn_ref, mn_ref, c_sc, n_sc, m_sc) = refs
    else:
        (q_ref, k_ref, v_ref, o_ref, gc_ref, gr_ref, bc_ref, br_ref, gh_ref,
         c_alias, h_ref, cn_ref, nn_ref, mn_ref, c_sc, n_sc, m_sc) = refs
    del c_alias
    c = pl.program_id(1)
    last = pl.num_programs(1) - 1

    @pl.when(c == 0)
    def _():
        if has_init:
            c_sc[...] = c0_ref[0].astype(F32)
            n_sc[...] = n0_ref[0].astype(F32)
            m_sc[...] = m0_ref[0].astype(F32)
        else:
            c_sc[...] = jnp.zeros_like(c_sc)
            n_sc[...] = jnp.zeros_like(n_sc)
            m_sc[...] = jnp.zeros_like(m_sc)

    for e in range(ne):
        rows = slice(e * lq, (e + 1) * lq)
        gcol = gc_ref[rows, :] + bc_ref[...]
        grow = gr_ref[:, e * lk:(e + 1) * lk] + br_ref[...]
        lf_col = _log_sigmoid(gcol)
        lf_row = _log_sigmoid(grow)
        if lq >= LANE:
            ri = lax.broadcasted_iota(jnp.int32, (lq, lq), 0)
            ci = lax.broadcasted_iota(jnp.int32, (lq, lq), 1)
            tril = jnp.where(ci <= ri, 1.0, 0.0).astype(F32)
            triu = jnp.where(ri <= ci, 1.0, 0.0).astype(F32)
            b_col = jnp.dot(tril, lf_col, precision=lax.Precision.HIGHEST, preferred_element_type=F32)
            b_row = jnp.dot(lf_row, triu, precision=lax.Precision.HIGHEST, preferred_element_type=F32)
        else:
            ri = lax.broadcasted_iota(jnp.int32, (lq, LANE), 0)
            ci = lax.broadcasted_iota(jnp.int32, (SUBLANE, lk), 1)
            b_col = jnp.zeros((lq, LANE), F32)
            b_row = jnp.zeros((SUBLANE, lk), F32)
            for s in range(lq):
                b_col = b_col + jnp.where(ri >= s, lf_col[s:s + 1, :], 0.0)
                b_row = b_row + jnp.where(ci >= s, lf_row[:, s:s + 1], 0.0)

        qi = lax.broadcasted_iota(jnp.int32, (lq, lk), 0)
        ki = lax.broadcasted_iota(jnp.int32, (lq, lk), 1)
        causal = ki <= qi
        kvalid = lax.broadcasted_iota(jnp.int32, (lk, 1), 0) < lq
        q_all = q_ref[rows, :]
        k_all = k_ref[rows, :]
        v_all = v_ref[rows, :]
        o_all = o_ref[rows, :]
        gh_all = gh_ref[...]
        m_prev_all = m_sc[e]

        def pad_keys(a):
            if lk == lq:
                return a
            return jnp.concatenate([a, jnp.zeros((lk - lq, a.shape[1]), a.dtype)], axis=0)

        for h in range(nh):
            sl = slice(h * dh, (h + 1) * dh)
            qh = q_all[:, sl].astype(F32)
            kh = pad_keys(k_all[:, sl]).astype(F32)
            vh = pad_keys(v_all[:, sl]).astype(F32)
            bq = b_col[:, nh + h:nh + h + 1]
            ig_k = pad_keys(gcol[:, h:h + 1])
            bk = pad_keys(bq)
            r_row = grow[h:h + 1, :] - b_row[nh + h:nh + h + 1, :]
            m_prev = m_prev_all[:, h:h + 1]
            log_d = jnp.where(causal, bq + r_row, -jnp.inf)
            log_p = bq + m_prev
            m_t = jnp.maximum(log_p, jnp.max(log_d, axis=-1, keepdims=True))
            w_intra = jnp.exp(log_d - m_t)
            w_prev = jnp.exp(log_p - m_t)
            qb = qh.astype(BF16)
            kb = kh.astype(BF16)
            s = _dot_nt(qb, kb) * w_intra
            cmat = c_sc[e, h]
            nrow = n_sc[e, h:h + 1, :]
            num = _dot(s.astype(BF16), vh.astype(BF16)) + w_prev * _dot_nt(qb, cmat.astype(BF16))
            den = jnp.sum(s, axis=-1, keepdims=True) + w_prev * jnp.sum(qh * nrow, axis=-1, keepdims=True)
            hh = num / jnp.maximum(jnp.abs(den), jnp.exp(-m_t))
            b_last = bq[lq - 1:lq, :]
            m_last = m_t[lq - 1:lq, :]
            w_last = jnp.where(kvalid, jnp.exp(b_last - bk + ig_k - m_last), 0.0)
            decay = w_prev[lq - 1:lq, :]
            c_new = decay * cmat + _dot_tn((vh * w_last).astype(BF16), kb)
            n_new = decay * nrow + jnp.sum(kh * w_last, axis=0, keepdims=True)
            c_sc[e, h] = c_new
            n_sc[e, h:h + 1, :] = n_new
            m_sc[e, :, h:h + 1] = m_last
            hm = jax.nn.sigmoid(o_all[:, sl].astype(F32)) * hh
            h_ref[rows, sl] = _rms(hm, gh_all[:, sl])

    @pl.when(c == last)
    def _():
        cn_ref[0] = c_sc[...]
        nn_ref[...] = n_sc[...]
        mn_ref[...] = m_sc[...]


def _mlstm(p, gates, gates_t, bias_c, bias_r, g_head, c_stack, layer, *, row_off, bsz, seq, chunk, col0, nh, dh,
           init=None, ne=1):
    w = nh * dh
    lq = chunk
    lk = max(chunk, LANE)
    nc = seq // chunk
    assert ne == 1 or (nc == 1 and bsz % ne == 0)
    rb0 = row_off // (ne * lq)
    has_init = init is not None

    def rowblk(b, c):
        return rb0 + b * nc + c

    in_specs = [
        pl.BlockSpec((ne * lq, w), lambda b, c: (rowblk(b, c), col0)),
        pl.BlockSpec((ne * lq, w), lambda b, c: (rowblk(b, c), col0 + 1)),
        pl.BlockSpec((ne * lq, w), lambda b, c: (rowblk(b, c), col0 + 2)),
        pl.BlockSpec((ne * lq, w), lambda b, c: (rowblk(b, c), col0 + 3)),
        pl.BlockSpec((ne * lq, LANE), lambda b, c: (rowblk(b, c), 0)),
        pl.BlockSpec((SUBLANE, ne * lk), lambda b, c: (0, rowblk(b, c))),
        pl.BlockSpec((1, LANE), lambda b, c: (0, 0)),
        pl.BlockSpec((SUBLANE, 1), lambda b, c: (0, 0)),
        pl.BlockSpec((1, w), lambda b, c: (0, 0)),
    ]
    args = [p, p, p, p, gates, gates_t, bias_c, bias_r, g_head.reshape(1, w)]
    if has_init:
        c0, n0, m0 = init
        in_specs += [
            pl.BlockSpec((1, ne, nh, dh, dh), lambda b, c: (layer, b, 0, 0, 0)),
            pl.BlockSpec((1, ne, nh, dh), lambda b, c: (layer, b, 0, 0)),
            pl.BlockSpec((1, ne, 1, nh), lambda b, c: (layer, b, 0, 0)),
        ]
        args += [c0, n0, m0.reshape(m0.shape[0], bsz, 1, nh)]
    in_specs.append(pl.BlockSpec(memory_space=pl.ANY))
    args.append(c_stack)
    out_specs = [
        pl.BlockSpec((ne * lq, w), lambda b, c: (b * nc + c, 0)),
        pl.BlockSpec((1, ne, nh, dh, dh), lambda b, c: (layer, b, 0, 0, 0)),
        pl.BlockSpec((ne, nh, dh), lambda b, c: (b, 0, 0)),
        pl.BlockSpec((ne, 1, nh), lambda b, c: (b, 0, 0)),
    ]
    out_shape = [
        jax.ShapeDtypeStruct((bsz * seq, w), F32),
        jax.ShapeDtypeStruct(c_stack.shape, F32),
        jax.ShapeDtypeStruct((bsz, nh, dh), F32),
        jax.ShapeDtypeStruct((bsz, 1, nh), F32),
    ]
    h, cn, nn, mn = pl.pallas_call(
        functools.partial(_mlstm_body, nh=nh, dh=dh, lq=lq, lk=lk, ne=ne, has_init=has_init),
        grid=(bsz // ne, nc),
        in_specs=in_specs,
        out_specs=out_specs,
        out_shape=out_shape,
        input_output_aliases={len(args) - 1: 1},
        scratch_shapes=[pltpu.VMEM((ne, nh, dh, dh), F32), pltpu.VMEM((ne, nh, dh), F32),
                        pltpu.VMEM((ne, 1, nh), F32)],
        compiler_params=_cparams(("parallel", "arbitrary")),
    )(*args)
    return h, (cn, nn, mn.reshape(bsz, nh))


def _bucket_np(dist):
    n = np.maximum(dist, 0)
    max_exact = REL_BUCKETS // 2
    nf = np.maximum(n, 1).astype(np.float32)
    large = max_exact + (np.log(nf / np.float32(max_exact)) / np.float32(math.log(REL_MAX_DIST / max_exact))
                         * np.float32(REL_BUCKETS - max_exact)).astype(np.int32)
    large = np.minimum(large, REL_BUCKETS - 1)
    return np.where(dist < 0, -1, np.where(n < max_exact, n, large)).astype(np.int32)


def _bias_body(tbl_ref, map_ref, o_ref, *, ntile):
    h = pl.program_id(0)
    for t in range(ntile):
        bm = map_ref[t]
        acc = jnp.full(bm.shape, NEG, F32)
        for b in range(REL_BUCKETS):
            acc = jnp.where(bm == b, tbl_ref[h, b], acc)
        o_ref[0, t] = acc


def _bias_tiles(rel_t, maps):
    nhead = rel_t.shape[0]
    ntile, r, c = maps.shape
    return pl.pallas_call(
        functools.partial(_bias_body, ntile=ntile),
        grid=(nhead,),
        in_specs=[
            pl.BlockSpec(memory_space=pltpu.SMEM),
            pl.BlockSpec((ntile, r, c), lambda h: (0, 0, 0)),
        ],
        out_specs=pl.BlockSpec((1, ntile, r, c), lambda h: (h, 0, 0, 0)),
        out_shape=jax.ShapeDtypeStruct((nhead, ntile, r, c), F32),
        compiler_params=_cparams(("arbitrary",)),
    )(rel_t, jnp.asarray(maps))


def _prompt_bias_maps():
    k = np.arange(MOBA_BLOCK)[:, None]
    q = np.arange(MOBA_BLOCK)[None, :]
    return np.stack([_bucket_np(q - k), _bucket_np(MOBA_BLOCK + q - k)])


def _sample_bias_maps(t_new):
    r = np.arange(t_new)[:, None]
    c = np.arange(MOBA_BLOCK)[None, :]
    own = np.where(c < t_new, r - c, -1)
    return np.stack([_bucket_np(MOBA_BLOCK + r - c), _bucket_np(own), _bucket_np(2 * MOBA_BLOCK + r - c + t_new)])


def _topk_select(scores):
    n = len(scores)
    sel = []
    for a in range(n):
        cnt = jnp.zeros(scores[a].shape, F32)
        for b in range(n):
            if b == a:
                continue
            ahead = (scores[b] >= scores[a]) if b < a else (scores[b] > scores[a])
            cnt = cnt + jnp.where(ahead, 1.0, 0.0)
        keep = cnt < MOBA_TOPK
        sel.append(keep)
    return sel


def _moba_p_body(tbl_ref, q_ref, k_ref, v_ref, bias_ref, o_ref, km_sc, vt_sc, lg_sc, *, nblk, dh, hps):
    hg = pl.program_id(1)
    j = pl.program_id(2)
    blk = MOBA_BLOCK

    @pl.when(j == 0)
    def _():
        km_sc[...] = jnp.zeros_like(km_sc)
        for hh in range(hps):
            sl = slice(hh * dh, (hh + 1) * dh)
            for b in range(nblk):
                kblk = k_ref[b * blk:(b + 1) * blk, sl].astype(F32)
                km_sc[hh, b:b + 1, :] = jnp.mean(kblk, axis=0, keepdims=True)
            vt_sc[hh] = v_ref[:, sl].astype(F32).T.astype(BF16)

    def select(hh):
        qs = q_ref[:, hh * dh:(hh + 1) * dh]
        st = _dot_nt(km_sc[hh].astype(BF16), qs)
        rowi = lax.broadcasted_iota(jnp.int32, st.shape, 0)
        st = jnp.where(rowi < j, st, -jnp.inf)
        cnt = jnp.zeros(st.shape, F32)
        for b in range(nblk):
            sb = st[b:b + 1, :]
            cnt = cnt + jnp.where(rowi > b, jnp.where(sb >= st, 1.0, 0.0), jnp.where(sb > st, 1.0, 0.0))
        return qs, jnp.where(cnt < MOBA_TOPK, jnp.where(rowi < j, 1.0, 0.0), 0.0)

    picked = [select(hh) for hh in range(hps)]

    def run(jj):
        for hh in range(hps):
            sl = slice(hh * dh, (hh + 1) * dh)
            qs, sel_t = picked[hh]
            far = tbl_ref[hg * hps + hh, REL_BUCKETS - 1]
            m = None
            for b in range(jj + 1):
                lg = _dot_nt(k_ref[b * blk:(b + 1) * blk, sl], qs)
                if b == jj:
                    lg = lg + bias_ref[hh, 0]
                elif b == jj - 1:
                    lg = lg + bias_ref[hh, 1]
                else:
                    lg = lg + far
                if b < jj:
                    lg = jnp.where(sel_t[b:b + 1, :] > 0.5, lg, NEG)
                lg_sc[hh, b] = lg
                tmax = jnp.max(lg, axis=0, keepdims=True)
                m = tmax if m is None else jnp.maximum(m, tmax)
            l = jnp.zeros_like(m)
            acc = jnp.zeros((dh, blk), F32)
            for b in range(jj + 1):
                pr = jnp.exp(lg_sc[hh, b] - m)
                l = l + jnp.sum(pr, axis=0, keepdims=True)
                acc = acc + _dot(vt_sc[hh, :, b * blk:(b + 1) * blk], pr.astype(BF16))
            o_ref[:, sl] = (acc / l).T

    for jj in range(nblk):
        pl.when(j == jj)(functools.partial(run, jj))


def _moba_prompt(rel_t, p, bias, *, bsz, seq, colq, colk, colv, nhead, dh):
    assert dh == LANE and seq % MOBA_BLOCK == 0 and MOBA_BLOCK >= REL_MAX_DIST
    nblk = seq // MOBA_BLOCK
    assert nblk <= SUBLANE
    ntile = bias.shape[1]
    hps = 2
    assert nhead % hps == 0 and colq % hps == 0 and colk % hps == 0 and colv % hps == 0
    cq, ck, cv = colq // hps, colk // hps, colv // hps
    return pl.pallas_call(
        functools.partial(_moba_p_body, nblk=nblk, dh=dh, hps=hps),
        grid=(bsz, nhead // hps, nblk),
        in_specs=[
            pl.BlockSpec(memory_space=pltpu.SMEM),
            pl.BlockSpec((MOBA_BLOCK, hps * dh), lambda b, h, j: (b * nblk + j, cq + h)),
            pl.BlockSpec((seq, hps * dh), lambda b, h, j: (b, ck + h)),
            pl.BlockSpec((seq, hps * dh), lambda b, h, j: (b, cv + h)),
            pl.BlockSpec((hps, ntile, MOBA_BLOCK, MOBA_BLOCK), lambda b, h, j: (h, 0, 0, 0)),
        ],
        out_specs=pl.BlockSpec((MOBA_BLOCK, hps * dh), lambda b, h, j: (b * nblk + j, h)),
        out_shape=jax.ShapeDtypeStruct((bsz * seq, nhead * dh), F32),
        scratch_shapes=[
            pltpu.VMEM((hps, 2 * SUBLANE, dh), F32),
            pltpu.VMEM((hps, dh, seq), BF16),
            pltpu.VMEM((hps, nblk, MOBA_BLOCK, MOBA_BLOCK), F32),
        ],
        compiler_params=_cparams(("parallel", "parallel", "arbitrary")),
    )(rel_t, p, p, p, bias)


def _moba_s_body(pt_ref, q_ref, kn_ref, vn_ref, bias_ref, *rest, npage, nhead, dh, t_new, page, bsz):
    kp = rest[:npage]
    vp = rest[npage:2 * npage]
    o_ref, p_sc, l_sc, acc_sc = rest[2 * npage:]
    s = pl.program_id(0)
    ppb = MOBA_BLOCK // page
    nblk = npage // ppb

    def heads(x):
        return jnp.stack([x[:, h * dh:(h + 1) * dh] for h in range(nhead)], axis=0)

    def pad_rows(x):
        return jnp.concatenate([x, jnp.zeros((page - t_new, x.shape[1]), x.dtype)], axis=0)

    @pl.when(s >= 1)
    def _():
        slot = (s + 1) % 2
        acc = acc_sc[slot]
        for pg in range(npage):
            acc = acc + jnp.einsum('htk,hkd->htd', p_sc[slot, pg].astype(BF16), vp[pg][0, 0].astype(BF16),
                                   preferred_element_type=F32)
        out = acc / l_sc[slot]
        for h in range(nhead):
            o_ref[:, h * dh:(h + 1) * dh] = out[h]

    @pl.when(s < bsz)
    def _():
        slot = s % 2
        q3 = heads(q_ref[...]).astype(BF16)
        lgs =[jnp.einsum('htd,hkd->htk', q3, kp[pg][0, 0].astype(BF16), preferred_element_type=F32)
               for pg in range(npage)]
        scores = []
        for b in range(nblk):
            tot = lgs[b * ppb]
            for r in range(1, ppb):
                tot = tot + lgs[b * ppb + r]
            scores.append(jnp.sum(tot, axis=-1, keepdims=True))
        sel = _topk_select(scores)
        far = bias_ref[:, 2]
        prev = bias_ref[:, 0]
        ml = []
        for pg in range(npage):
            b = pg // ppb
            r = pg % ppb
            bias = prev[:, :, r * page:(r + 1) * page] if b == nblk - 1 else far[:, :, :page]
            ml.append(jnp.where(sel[b], lgs[pg] + bias, NEG))
        kn3 = heads(pad_rows(kn_ref[...])).astype(BF16)
        vn3 = heads(pad_rows(vn_ref[...])).astype(BF16)
        lo = jnp.einsum('htd,hkd->htk', q3, kn3, preferred_element_type=F32) + bias_ref[:, 1][:, :, :page]
        m = jnp.max(lo, axis=-1, keepdims=True)
        for x in ml:
            m = jnp.maximum(m, jnp.max(x, axis=-1, keepdims=True))
        eo = jnp.exp(lo - m)
        l = jnp.sum(eo, axis=-1, keepdims=True)
        for pg in range(npage):
            e = jnp.exp(ml[pg] - m)
            l = l + jnp.sum(e, axis=-1, keepdims=True)
            p_sc[slot, pg] = e
        l_sc[slot] = l
        acc_sc[slot] = jnp.einsum('htk,hkd->htd', eo.astype(BF16), vn3, preferred_element_type=F32)


def _moba_sample(p, bias, kpool, vpool, layer, page_table, *, row_off, bsz, t_new, colq, colk, colv, nhead, dh):
    npage = page_table.shape[1]
    page = kpool.shape[3]
    assert dh == LANE and page == LANE and MOBA_BLOCK % page == 0
    assert (npage * page) % MOBA_BLOCK == 0 and npage * page >= MOBA_BLOCK and t_new <= page
    w = nhead * dh
    rb0 = row_off // t_new
    ntile = bias.shape[1]
    pt = page_table.reshape(-1).astype(jnp.int32)
    assert colq % nhead == 0 and colk % nhead == 0 and colv % nhead == 0
    colq, colk, colv = colq // nhead, colk // nhead, colv // nhead

    def cur(s):
        return jnp.minimum(s, bsz - 1)

    def prev(s):
        return jnp.maximum(s - 1, 0)

    def kmap(pg):
        return lambda s, pt_ref: (layer, pt_ref[cur(s) * npage + pg], 0, 0, 0)

    def vmap_(pg):
        return lambda s, pt_ref: (layer, pt_ref[prev(s) * npage + pg], 0, 0, 0)

    in_specs = [
        pl.BlockSpec((t_new, w), lambda s, pt_ref: (rb0 + cur(s), colq)),
        pl.BlockSpec((t_new, w), lambda s, pt_ref: (rb0 + cur(s), colk)),
        pl.BlockSpec((t_new, w), lambda s, pt_ref: (rb0 + cur(s), colv)),
        pl.BlockSpec((nhead, ntile, t_new, MOBA_BLOCK), lambda s, pt_ref: (0, 0, 0, 0)),
    ]
    in_specs += [pl.BlockSpec((1, 1, nhead, page, dh), kmap(pg)) for pg in range(npage)]
    in_specs += [pl.BlockSpec((1, 1, nhead, page, dh), vmap_(pg)) for pg in range(npage)]
    grid_spec = pltpu.PrefetchScalarGridSpec(
        num_scalar_prefetch=1,
        grid=(bsz + 1,),
        in_specs=in_specs,
        out_specs=pl.BlockSpec((t_new, w), lambda s, pt_ref: (prev(s), 0)),
        scratch_shapes=[
            pltpu.VMEM((2, npage, nhead, t_new, page), F32),
            pltpu.VMEM((2, nhead, t_new, 1), F32),
            pltpu.VMEM((2, nhead, t_new, dh), F32),
        ],
    )
    return pl.pallas_call(
        functools.partial(_moba_s_body, npage=npage, nhead=nhead, dh=dh, t_new=t_new, page=page, bsz=bsz),
        grid_spec=grid_spec,
        out_shape=jax.ShapeDtypeStruct((bsz * t_new, w), F32),
        compiler_params=_cparams(("arbitrary",)),
    )(pt, p, p, p, bias, *([kpool] * npage), *([vpool] * npage))


def _mix_body(*refs, starts):
    ns = len(starts) - 1
    hm_refs, hb_refs = refs[:ns], refs[ns:2 * ns]
    ga_ref, gb_ref, x_ref, wa_ref, wb_ref, wo_ref, o_ref = refs[2 * ns:]
    i = pl.program_id(0)
    a = _dot(_seg_load(hm_refs, starts, i).astype(BF16), wa_ref[...])
    b = _dot(_seg_load(hb_refs, starts, i).astype(BF16), wb_ref[...])
    merged = jax.nn.sigmoid(ga_ref[...].astype(F32)) * a + jax.nn.sigmoid(gb_ref[...].astype(F32)) * b
    o_ref[...] = x_ref[...] + _dot(merged.astype(BF16), wo_ref[...])


def _mix(hms, hbs, p, x, wa, wb, wo):
    m, d = x.shape
    wm = hms[0].shape[1]
    wbw = hbs[0].shape[1]
    rows = [h.shape[0] for h in hms]
    assert rows == [h.shape[0] for h in hbs] and sum(rows) == m
    tm = _pick(math.gcd(*rows), (256, 128, 64, 32, 16, 8))
    starts = _seg_starts(rows, tm)
    const = dict(pipeline_mode=pl.Buffered(1))
    return pl.pallas_call(
        functools.partial(_mix_body, starts=starts),
        grid=(m // tm,),
        in_specs=[_seg_spec(starts, k, tm, wm) for k in range(len(rows))]
        + [_seg_spec(starts, k, tm, wbw) for k in range(len(rows))]
        + [
            pl.BlockSpec((tm, d), lambda i: (i, 0)),
            pl.BlockSpec((tm, d), lambda i: (i, 1)),
            pl.BlockSpec((tm, d), lambda i: (i, 0)),
            pl.BlockSpec((wm, d), lambda i: (0, 0), **const),
            pl.BlockSpec((wbw, d), lambda i: (0, 0), **const),
            pl.BlockSpec((d, d), lambda i: (0, 0), **const),
        ],
        out_specs=pl.BlockSpec((tm, d), lambda i: (i, 0)),
        out_shape=jax.ShapeDtypeStruct((m, d), F32),
        compiler_params=_cparams(("parallel",)),
    )(*hms, *hbs, p, p, x, wa, wb, wo)


def kernel(x_prompt, x_sample, cache_k, cache_v, state_C, state_n, state_m, page_table, g_ff1, w_ff1_gate,
           w_ff1_up, w_ff1_down, g_mix, w_in, b_ig, b_fg, g_head, w_a, w_b, w_out, g_ff2, w_ff2_gate,
           w_ff2_up, w_ff2_down, rel_bias_table, g_final):
    bp, seq, d = x_prompt.shape
    bd, t_new, _ = x_sample.shape
    depth = g_ff1.shape[0]
    nh_m = b_ig.shape[1]
    w_m = w_a.shape[1]
    dh_m = w_m // nh_m
    nh_b = rel_bias_table.shape[1]
    w_bw = w_b.shape[1]
    dh_b = w_bw // nh_b
    page = cache_k.shape[3]
    mp = bp * seq
    sdt = state_C.dtype
    assert 2 * nh_m <= SUBLANE and (2 * d) % w_m == 0 and (2 * d + 4 * w_m) % LANE == 0
    assert mp % t_new == 0 and seq % MLSTM_PROMPT_CHUNK == 0 and seq % page == 0

    o_gate = 4 * w_m
    o_moba = o_gate + 2 * nh_m
    o_ga = o_moba + 3 * w_bw
    col_m = (2 * d) // w_m
    col_b = (2 * d + 4 * w_m) // dh_b
    colq_b, colk_b, colv_b = col_b, col_b + nh_b, col_b + 2 * nh_b

    n_slab = 2 * d + 4 * w_m + 3 * w_bw
    col_scale = np.ones((1, n_slab), np.float32)
    col_scale[:, 2 * d + w_m:2 * d + 2 * w_m] = dh_m ** -0.5
    col_scale[:, 2 * d + 4 * w_m:2 * d + 4 * w_m + w_bw] = dh_b ** -0.5
    col_scale = jnp.asarray(col_scale)

    rel_t = rel_bias_table.T.astype(F32)
    bias_p = _bias_tiles(rel_t, _prompt_bias_maps())
    bias_s = _bias_tiles(rel_t, _sample_bias_maps(t_new))

    ms_rows = bd * t_new
    xs = [x_prompt.reshape(mp, d), x_sample.reshape(ms_rows, d)]
    caches = (jnp.zeros((depth, mp // page, nh_b, page, dh_b), F32), jnp.zeros((depth, mp // page, nh_b, page, dh_b), F32),
              jnp.zeros((depth, bd, nh_b, t_new, dh_b), F32), jnp.zeros((depth, bd, nh_b, t_new, dh_b), F32))
    c_prompt = jnp.zeros((depth, bp, nh_m, dh_m, dh_m), F32)
    c_sample = jnp.zeros((depth, bd, nh_m, dh_m, dh_m), F32)
    np_l, mp_l, ns_l, ms_l = [], [], [], []
    for l in range(depth):
        wi = w_in[l]
        w_main = jnp.concatenate([wi[:, o_ga:], wi[:, :o_gate], wi[:, o_moba:o_ga]], axis=1).astype(BF16)
        w_gate = jnp.pad(wi[:, o_gate:o_moba], ((0, 0), (0, LANE - 2 * nh_m))).astype(BF16)
        gate_bias = jnp.concatenate([b_ig[l], b_fg[l]]).astype(F32)
        bias_c = jnp.pad(gate_bias, (0, LANE - 2 * nh_m)).reshape(1, LANE)
        bias_r = jnp.pad(gate_bias, (0, SUBLANE - 2 * nh_m)).reshape(SUBLANE, 1)

        x = _ffn(xs, g_ff1[l], _cast_layer(w_ff1_gate, l), _cast_layer(w_ff1_up, l), _cast_layer(w_ff1_down, l))
        p, gates, gt_p, caches = _proj(x, g_mix[l], w_main, w_gate, col_scale, caches, l, mp=mp, nhead=nh_b,
                                       dh=dh_b, page=page, t_new=t_new)
        p_s = p[mp:].astype(F32)

        gt_s = gt_p[:, mp:].reshape(SUBLANE, bd, t_new)
        gt_s = jnp.pad(gt_s, ((0, 0), (0, 0), (0, LANE - t_new))).reshape(SUBLANE, bd * LANE)

        mk = dict(col0=col_m, nh=nh_m, dh=dh_m)
        hm_p, (c_prompt, npp, mpp) = _mlstm(p, gates, gt_p, bias_c, bias_r, g_head[l], c_prompt, l, row_off=0,
                                            bsz=bp, seq=seq, chunk=MLSTM_PROMPT_CHUNK, **mk)
        init = (state_C, state_n, state_m)
        hm_s, (c_sample, ns, ms) = _mlstm(p_s, gates[mp:], gt_s, bias_c, bias_r, g_head[l], c_sample, l, row_off=0,
                                          bsz=bd, seq=t_new, chunk=t_new, init=init, ne=2 if bd % 2 == 0 else 1, **mk)

        bk = dict(colq=colq_b, colk=colk_b, colv=colv_b, nhead=nh_b, dh=dh_b)
        hb_p = _moba_prompt(rel_t, p, bias_p, bsz=bp, seq=seq, **bk)
        hb_s = _moba_sample(p_s, bias_s, cache_k, cache_v, l, page_table, row_off=0, bsz=bd, t_new=t_new, **bk)

        x = _mix([hm_p, hm_s], [hb_p, hb_s], p, x, _cast_layer(w_a, l), _cast_layer(w_b, l), _cast_layer(w_out, l))
        ffn2 = (g_ff2[l], _cast_layer(w_ff2_gate, l), _cast_layer(w_ff2_up, l), _cast_layer(w_ff2_down, l))
        if l == depth - 1:
            y_prompt, y_sample = _ffn([x], *ffn2, g_final=g_final, out_rows=(mp, ms_rows))
        else:
            xs = [_ffn([x], *ffn2)]

        np_l.append(npp.astype(sdt))
        mp_l.append(mpp.astype(sdt))
        ns_l.append(ns.astype(sdt))
        ms_l.append(ms.astype(sdt))

    k_pages, v_pages, k_rows, v_rows = caches
    page_shape = (depth, bp, seq // page, nh_b, page, dh_b)
    return (y_prompt.reshape(bp, seq, d), y_sample.reshape(bd, t_new, d),
            k_pages.reshape(page_shape), v_pages.reshape(page_shape), k_rows, v_rows,
            c_prompt.astype(sdt), jnp.stack(np_l), jnp.stack(mp_l), c_sample.astype(sdt), jnp.stack(ns_l), jnp.stack(ms_l))
```

```python
import functools
import math

import numpy as np
import jax
import jax.numpy as jnp
from jax import lax
from jax.experimental import pallas as pl
from jax.experimental.pallas import tpu as pltpu

F32 = jnp.float32
BF16 = jnp.bfloat16

NORM_EPS = 1e-6
MOBA_BLOCK = 256
MOBA_TOPK = 3
REL_BUCKETS = 32
REL_MAX_DIST = 128
MLSTM_PROMPT_CHUNK = 256
LANE = 128
SUBLANE = 8
NEG = -1e30
VMEM_LIMIT = 56 * 1024 * 1024


def _pick(n, cands):
    for c in cands:
        if n % c == 0:
            return c
    return n


def _cparams(sem):
    return pltpu.CompilerParams(dimension_semantics=sem, vmem_limit_bytes=VMEM_LIMIT)


def _rms(x, g):
    return x * lax.rsqrt(jnp.mean(x * x, axis=-1, keepdims=True) + NORM_EPS) * g


def _dot(a, b):
    return jnp.dot(a, b, preferred_element_type=F32)


def _dot_nt(a, b):
    return lax.dot_general(a, b, (((1,), (1,)), ((), ())), preferred_element_type=F32)


def _dot_tn(a, b):
    return lax.dot_general(a, b, (((0,), (0,)), ((), ())), preferred_element_type=F32)


CAST_BLOCK_BYTES = 6 * 1024 * 1024


def _cast_body(w_ref, o_ref):
    o_ref[...] = w_ref[0].astype(o_ref.dtype)


def _cast_layer(w, layer):
    _, rows, cols = w.shape
    tr = rows
    while tr % 2 == 0 and tr // 2 >= 2 * SUBLANE and tr * cols * 4 > CAST_BLOCK_BYTES:
        tr //= 2
    return pl.pallas_call(
        _cast_body,
        grid=(rows // tr,),
        in_specs=[pl.BlockSpec((1, tr, cols), lambda i: (layer, i, 0))],
        out_specs=pl.BlockSpec((tr, cols), lambda i: (i, 0)),
        out_shape=jax.ShapeDtypeStruct((rows, cols), BF16),
        compiler_params=_cparams(("parallel",)),
    )(w)


def _seg_starts(rows, tm):
    starts = [0]
    for r in rows:
        assert r % tm == 0
        starts.append(starts[-1] + r // tm)
    return starts


def _seg_spec(starts, k, tm, width, col=0, ngrid=1, **kw):
    lo, n = starts[k], starts[k + 1] - starts[k]
    if ngrid == 1:
        return pl.BlockSpec((tm, width), lambda i: (jnp.clip(i - lo, 0, n - 1), col), **kw)
    return pl.BlockSpec((tm, width), lambda i, j: (jnp.clip(i - lo, 0, n - 1), col), **kw)


def _seg_load(refs, starts, i):
    x = refs[0][...]
    for k in range(1, len(refs)):
        x = jnp.where(i >= starts[k], refs[k][...], x)
    return x


def _ffn_body(*refs, in_starts, out_starts, final, has_next):
    nx, no = len(in_starts) - 1, len(out_starts) - 1
    x_refs = refs[:nx]
    g_ref, wg_ref, wu_ref, wd_ref = refs[nx:nx + 4]
    rest = refs[nx + 4:]
    if final:
        gf_ref, rest = rest[0], rest[1:]
    if has_next:
        gn_ref, rest = rest[0], rest[1:]
    o_refs, rest = rest[:no], rest[no:]
    if has_next:
        hnext_ref, rest = rest[0], rest[1:]
    hn_ref = rest[0]
    acc_ref = rest[1] if no > 1 else o_refs[0]
    i = pl.program_id(0)
    f = pl.program_id(1)

    @pl.when(f == 0)
    def _():
        hn_ref[...] = _rms(_seg_load(x_refs, in_starts, i), g_ref[...]).astype(BF16)
        acc_ref[...] = jnp.zeros_like(acc_ref)

    h = hn_ref[...]
    g = _dot(h, wg_ref[...])
    u = _dot(h, wu_ref[...])
    a = (g * jax.nn.sigmoid(g) * u).astype(BF16)
    acc_ref[...] += _dot(a, wd_ref[...])

    @pl.when(f == pl.num_programs(1) - 1)
    def _():
        y = _seg_load(x_refs, in_starts, i) + 0.5 * acc_ref[...]
        if final:
            y = _rms(y, gf_ref[...])
        if has_next:
            hnext_ref[...] = _rms(y, gn_ref[...]).astype(BF16)
        if no == 1:
            o_refs[0][...] = y
        else:
            for k in range(no):
                def store(k=k):
                    o_refs[k][...] = y
                pl.when(jnp.logical_and(i >= out_starts[k], i < out_starts[k + 1]))(store)


def _ffn(xs, g, wg, wu, wd, g_final=None, out_rows=None, g_next=None):
    d = xs[0].shape[1]
    rows_in = [x.shape[0] for x in xs]
    m = sum(rows_in)
    rows_out = list(out_rows) if out_rows is not None else [m]
    ff = wg.shape[1]
    tm = _pick(math.gcd(*rows_in, *rows_out), (512, 256, 128, 64, 32, 16, 8))
    tf = _pick(ff, (512, 256, 128))
    final = g_final is not None
    in_starts = _seg_starts(rows_in, tm)
    out_starts = _seg_starts(rows_out, tm)
    in_specs = [_seg_spec(in_starts, k, tm, d, ngrid=2) for k in range(len(xs))]
    in_specs += [
        pl.BlockSpec((1, d), lambda i, f: (0, 0)),
        pl.BlockSpec((d, tf), lambda i, f: (0, f)),
        pl.BlockSpec((d, tf), lambda i, f: (0, f)),
        pl.BlockSpec((tf, d), lambda i, f: (f, 0)),
    ]
    args = [*xs, g.reshape(1, d), wg, wu, wd]
    if final:
        in_specs.append(pl.BlockSpec((1, d), lambda i, f: (0, 0)))
        args.append(g_final.reshape(1, d))
    has_next = g_next is not None
    out_specs = [_seg_spec(out_starts, k, tm, d, ngrid=2) for k in range(len(rows_out))]
    out_shape = [jax.ShapeDtypeStruct((r, d), F32) for r in rows_out]
    if has_next:
        assert out_rows is None
        in_specs.append(pl.BlockSpec((1, d), lambda i, f: (0, 0)))
        args.append(g_next.reshape(1, d))
        out_specs.append(pl.BlockSpec((tm, d), lambda i, f: (i, 0)))
        out_shape.append(jax.ShapeDtypeStruct((m, d), BF16))
    scratch = [pltpu.VMEM((tm, d), BF16)]
    if len(rows_out) > 1:
        scratch.append(pltpu.VMEM((tm, d), F32))
    outs = pl.pallas_call(
        functools.partial(_ffn_body, in_starts=in_starts, out_starts=out_starts, final=final, has_next=has_next),
        grid=(m // tm, ff // tf),
        in_specs=in_specs,
        out_specs=out_specs,
        out_shape=out_shape,
        scratch_shapes=scratch,
        compiler_params=_cparams(("parallel", "arbitrary")),
    )(*args)
    return outs if (out_rows is not None or has_next) else outs[0]


def _proj_body(hn_ref, w_ref, wgate_ref, cs_ref, kp_in, vp_in, ks_in, vs_in,
               p_ref, gate_ref, gatet_ref, kp_ref, vp_ref, ks_ref, vs_ref,
               *, j_k, j_v, npt, nhead, dh, page, t_new):
    del kp_in, vp_in, ks_in, vs_in
    i = pl.program_id(0)
    j = pl.program_id(1)
    tm = hn_ref.shape[0]

    @pl.when(j == 0)
    def _():
        gate = _dot(hn_ref[...], wgate_ref[...])
        gate_ref[...] = gate
        gatet_ref[...] = gate.T[:SUBLANE, :]

    acc = _dot(hn_ref[...], w_ref[...])
    p_ref[...] = (acc * cs_ref[...]).astype(BF16)

    def put(pages_ref, rows_ref):
        @pl.when(i < npt)
        def _():
            for pg in range(tm // page):
                for h in range(nhead):
                    pages_ref[0, pg, h] = acc[pg * page:(pg + 1) * page, h * dh:(h + 1) * dh]

        @pl.when(i >= npt)
        def _():
            for h in range(nhead):
                rows_ref[0, :, h] = acc[:, h * dh:(h + 1) * dh].reshape(tm // t_new, t_new, dh)

    pl.when(j == j_k)(functools.partial(put, kp_ref, ks_ref))
    pl.when(j == j_v)(functools.partial(put, vp_ref, vs_ref))


def _proj(hn, w_main, w_gate, col_scale, caches, layer, *, mp, nhead, dh, page, t_new):
    m, d = hn.shape
    n = w_main.shape[1]
    w = nhead * dh
    tn = w
    tm = _pick(math.gcd(mp, m - mp), (1024, 512, 256, 128))
    assert n % tn == 0 and tm % page == 0 and tm % t_new == 0
    j_k = (n - 2 * w) // tn
    npt = mp // tm
    nst = (m - mp) // tm
    once = dict(pipeline_mode=pl.Buffered(1))
    alias = pl.BlockSpec(memory_space=pl.ANY)
    pages_spec = pl.BlockSpec((1, tm // page, nhead, page, dh),
                              lambda i, j: (layer, jnp.minimum(i, npt - 1), 0, 0, 0), **once)
    rows_spec = pl.BlockSpec((1, tm // t_new, nhead, t_new, dh),
                             lambda i, j: (layer, jnp.clip(i - npt, 0, nst - 1), 0, 0, 0), **once)
    outs = pl.pallas_call(
        functools.partial(_proj_body, j_k=j_k, j_v=j_k + 1, npt=npt, nhead=nhead, dh=dh, page=page, t_new=t_new),
        grid=(m // tm, n // tn),
        in_specs=[
            pl.BlockSpec((tm, d), lambda i, j: (i, 0)),
            pl.BlockSpec((d, tn), lambda i, j: (0, j)),
            pl.BlockSpec((d, LANE), lambda i, j: (0, 0)),
            pl.BlockSpec((1, tn), lambda i, j: (0, j)),
            alias, alias, alias, alias,
        ],
        out_specs=[
            pl.BlockSpec((tm, tn), lambda i, j: (i, j)),
            pl.BlockSpec((tm, LANE), lambda i, j: (i, 0)),
            pl.BlockSpec((SUBLANE, tm), lambda i, j: (0, i)),
            pages_spec, pages_spec, rows_spec, rows_spec,
        ],
        out_shape=[jax.ShapeDtypeStruct((m, n), BF16), jax.ShapeDtypeStruct((m, LANE), F32),
                   jax.ShapeDtypeStruct((SUBLANE, m), F32)]
        + [jax.ShapeDtypeStruct(c.shape, c.dtype) for c in caches],
        input_output_aliases={4: 3, 5: 4, 6: 5, 7: 6},
        compiler_params=_cparams(("parallel", "arbitrary")),
    )(hn, w_main, w_gate, col_scale, *caches)
    return outs[0], outs[1], outs[2], tuple(outs[3:])


def _log_sigmoid(x):
    return jnp.minimum(x, 0.0) - jnp.log(1.0 + jnp.exp(-jnp.abs(x)))


def _mlstm_body(*refs, nh, dh, lq, lk, ne, has_init):
    if has_init:
        (q_ref, k_ref, v_ref, o_ref, gc_ref, gr_ref, bc_ref, br_ref, gh_ref,
         c0_ref, n0_ref, m0_ref, c_alias, h_ref, cn_ref, nn_ref, mn_ref, c_sc, n_sc, m_sc) = refs
    else:
        (q_ref, k_ref, v_ref, o_ref, gc_ref, gr_ref, bc_ref, br_ref, gh_ref,
         c_alias, h_ref, cn_ref, nn_ref, mn_ref, c_sc, n_sc, m_sc) = refs
    del c_alias
    c = pl.program_id(1)
    last = pl.num_programs(1) - 1

    @pl.when(c == 0)
    def _():
        if has_init:
            c_sc[...] = c0_ref[0].astype(F32)
            n_sc[...] = n0_ref[0].astype(F32)
            m_sc[...] = m0_ref[0].astype(F32)
        else:
            c_sc[...] = jnp.zeros_like(c_sc)
            n_sc[...] = jnp.zeros_like(n_sc)
            m_sc[...] = jnp.zeros_like(m_sc)

    for e in range(ne):
        rows = slice(e * lq, (e + 1) * lq)
        gcol = gc_ref[rows, :] + bc_ref[...]
        grow = gr_ref[:, e * lk:(e + 1) * lk] + br_ref[...]
        lf_col = _log_sigmoid(gcol)
        lf_row = _log_sigmoid(grow)
        if lq >= LANE:
            ri = lax.broadcasted_iota(jnp.int32, (lq, lq), 0)
            ci = lax.broadcasted_iota(jnp.int32, (lq, lq), 1)
            tril = jnp.where(ci <= ri, 1.0, 0.0).astype(F32)
            triu = jnp.where(ri <= ci, 1.0, 0.0).astype(F32)
            b_col = jnp.dot(tril, lf_col, precision=lax.Precision.HIGHEST, preferred_element_type=F32)
            b_row = jnp.dot(lf_row, triu, precision=lax.Precision.HIGHEST, preferred_element_type=F32)
        else:
            ri = lax.broadcasted_iota(jnp.int32, (lq, LANE), 0)
            ci = lax.broadcasted_iota(jnp.int32, (SUBLANE, lk), 1)
            b_col = jnp.zeros((lq, LANE), F32)
            b_row = jnp.zeros((SUBLANE, lk), F32)
            for s in range(lq):
                b_col = b_col + jnp.where(ri >= s, lf_col[s:s + 1, :], 0.0)
                b_row = b_row + jnp.where(ci >= s, lf_row[:, s:s + 1], 0.0)

        qi = lax.broadcasted_iota(jnp.int32, (lq, lk), 0)
        ki = lax.broadcasted_iota(jnp.int32, (lq, lk), 1)
        causal = ki <= qi
        kvalid = lax.broadcasted_iota(jnp.int32, (lk, 1), 0) < lq
        q_all = q_ref[rows, :]
        k_all = k_ref[rows, :]
        v_all = v_ref[rows, :]
        o_all = o_ref[rows, :]
        gh_all = gh_ref[...]
        m_prev_all = m_sc[e]

        def pad_keys(a):
            if lk == lq:
                return a
            return jnp.concatenate([a, jnp.zeros((lk - lq, a.shape[1]), a.dtype)], axis=0)

        for h in range(nh):
            sl = slice(h * dh, (h + 1) * dh)
            qh = q_all[:, sl].astype(F32)
            kh = pad_keys(k_all[:, sl]).astype(F32)
            vh = pad_keys(v_all[:, sl]).astype(F32)
            bq = b_col[:, nh + h:nh + h + 1]
            ig_k = pad_keys(gcol[:, h:h + 1])
            bk = pad_keys(bq)
            r_row = grow[h:h + 1, :] - b_row[nh + h:nh + h + 1, :]
            m_prev = m_prev_all[:, h:h + 1]
            log_d = jnp.where(causal, bq + r_row, -jnp.inf)
            log_p = bq + m_prev
            m_t = jnp.maximum(log_p, jnp.max(log_d, axis=-1, keepdims=True))
            w_intra = jnp.exp(log_d - m_t)
            w_prev = jnp.exp(log_p - m_t)
            qb = qh.astype(BF16)
            kb = kh.astype(BF16)
            s = _dot_nt(qb, kb) * w_intra
            cmat = c_sc[e, h]
            nrow = n_sc[e, h:h + 1, :]
            num = _dot(s.astype(BF16), vh.astype(BF16)) + w_prev * _dot_nt(qb, cmat.astype(BF16))
            den = jnp.sum(s, axis=-1, keepdims=True) + w_prev * jnp.sum(qh * nrow, axis=-1, keepdims=True)
            hh = num / jnp.maximum(jnp.abs(den), jnp.exp(-m_t))
            b_last = bq[lq - 1:lq, :]
            m_last = m_t[lq - 1:lq, :]
            w_last = jnp.where(kvalid, jnp.exp(b_last - bk + ig_k - m_last), 0.0)
            decay = w_prev[lq - 1:lq, :]
            c_new = decay * cmat + _dot_tn((vh * w_last).astype(BF16), kb)
            n_new = decay * nrow + jnp.sum(kh * w_last, axis=0, keepdims=True)
            c_sc[e, h] = c_new
            n_sc[e, h:h + 1, :] = n_new
            m_sc[e, :, h:h + 1] = m_last
            hm = jax.nn.sigmoid(o_all[:, sl].astype(F32)) * hh
            h_ref[rows, sl] = _rms(hm, gh_all[:, sl])

    @pl.when(c == last)
    def _():
        cn_ref[0] = c_sc[...]
        nn_ref[...] = n_sc[...]
        mn_ref[...] = m_sc[...]


def _mlstm(p, gates, gates_t, bias_c, bias_r, g_head, c_stack, layer, *, row_off, bsz, seq, chunk, col0, nh, dh,
           init=None, ne=1):
    w = nh * dh
    lq = chunk
    lk = max(chunk, LANE)
    nc = seq // chunk
    assert ne == 1 or (nc == 1 and bsz % ne == 0)
    rb0 = row_off // (ne * lq)
    has_init = init is not None

    def rowblk(b, c):
        return rb0 + b * nc + c

    in_specs = [
        pl.BlockSpec((ne * lq, w), lambda b, c: (rowblk(b, c), col0)),
        pl.BlockSpec((ne * lq, w), lambda b, c: (rowblk(b, c), col0 + 1)),
        pl.BlockSpec((ne * lq, w), lambda b, c: (rowblk(b, c), col0 + 2)),
        pl.BlockSpec((ne * lq, w), lambda b, c: (rowblk(b, c), col0 + 3)),
        pl.BlockSpec((ne * lq, LANE), lambda b, c: (rowblk(b, c), 0)),
        pl.BlockSpec((SUBLANE, ne * lk), lambda b, c: (0, rowblk(b, c))),
        pl.BlockSpec((1, LANE), lambda b, c: (0, 0)),
        pl.BlockSpec((SUBLANE, 1), lambda b, c: (0, 0)),
        pl.BlockSpec((1, w), lambda b, c: (0, 0)),
    ]
    args = [p, p, p, p, gates, gates_t, bias_c, bias_r, g_head.reshape(1, w)]
    if has_init:
        c0, n0, m0 = init
        in_specs += [
            pl.BlockSpec((1, ne, nh, dh, dh), lambda b, c: (layer, b, 0, 0, 0)),
            pl.BlockSpec((1, ne, nh, dh), lambda b, c: (layer, b, 0, 0)),
            pl.BlockSpec((1, ne, 1, nh), lambda b, c: (layer, b, 0, 0)),
        ]
        args += [c0, n0, m0.reshape(m0.shape[0], bsz, 1, nh)]
    in_specs.append(pl.BlockSpec(memory_space=pl.ANY))
    args.append(c_stack)
    out_specs = [
        pl.BlockSpec((ne * lq, w), lambda b, c: (b * nc + c, 0)),
        pl.BlockSpec((1, ne, nh, dh, dh), lambda b, c: (layer, b, 0, 0, 0)),
        pl.BlockSpec((ne, nh, dh), lambda b, c: (b, 0, 0)),
        pl.BlockSpec((ne, 1, nh), lambda b, c: (b, 0, 0)),
    ]
    out_shape = [
        jax.ShapeDtypeStruct((bsz * seq, w), F32),
        jax.ShapeDtypeStruct(c_stack.shape, F32),
        jax.ShapeDtypeStruct((bsz, nh, dh), F32),
        jax.ShapeDtypeStruct((bsz, 1, nh), F32),
    ]
    h, cn, nn, mn = pl.pallas_call(
        functools.partial(_mlstm_body, nh=nh, dh=dh, lq=lq, lk=lk, ne=ne, has_init=has_init),
        grid=(bsz // ne, nc),
        in_specs=in_specs,
        out_specs=out_specs,
        out_shape=out_shape,
        input_output_aliases={len(args) - 1: 1},
        scratch_shapes=[pltpu.VMEM((ne, nh, dh, dh), F32), pltpu.VMEM((ne, nh, dh), F32),
                        pltpu.VMEM((ne, 1, nh), F32)],
        compiler_params=_cparams(("parallel", "arbitrary")),
    )(*args)
    return h, (cn, nn, mn.reshape(bsz, nh))


def _bucket_np(dist):
    n = np.maximum(dist, 0)
    max_exact = REL_BUCKETS // 2
    nf = np.maximum(n, 1).astype(np.float32)
    large = max_exact + (np.log(nf / np.float32(max_exact)) / np.float32(math.log(REL_MAX_DIST / max_exact))
                         * np.float32(REL_BUCKETS - max_exact)).astype(np.int32)
    large = np.minimum(large, REL_BUCKETS - 1)
    return np.where(dist < 0, -1, np.where(n < max_exact, n, large)).astype(np.int32)


def _bias_body(tbl_ref, map_ref, o_ref, *, ntile):
    h = pl.program_id(0)
    for t in range(ntile):
        bm = map_ref[t]
        acc = jnp.full(bm.shape, NEG, F32)
        for b in range(REL_BUCKETS):
            acc = jnp.where(bm == b, tbl_ref[h, b], acc)
        o_ref[0, t] = acc


def _bias_tiles(rel_t, maps):
    nhead = rel_t.shape[0]
    ntile, r, c = maps.shape
    return pl.pallas_call(
        functools.partial(_bias_body, ntile=ntile),
        grid=(nhead,),
        in_specs=[
            pl.BlockSpec(memory_space=pltpu.SMEM),
            pl.BlockSpec((ntile, r, c), lambda h: (0, 0, 0)),
        ],
        out_specs=pl.BlockSpec((1, ntile, r, c), lambda h: (h, 0, 0, 0)),
        out_shape=jax.ShapeDtypeStruct((nhead, ntile, r, c), F32),
        compiler_params=_cparams(("arbitrary",)),
    )(rel_t, jnp.asarray(maps))


def _prompt_bias_maps():
    k = np.arange(MOBA_BLOCK)[:, None]
    q = np.arange(MOBA_BLOCK)[None, :]
    return np.stack([_bucket_np(q - k), _bucket_np(MOBA_BLOCK + q - k)])


def _sample_bias_maps(t_new):
    r = np.arange(t_new)[:, None]
    c = np.arange(MOBA_BLOCK)[None, :]
    own = np.where(c < t_new, r - c, -1)
    return np.stack([_bucket_np(MOBA_BLOCK + r - c), _bucket_np(own), _bucket_np(2 * MOBA_BLOCK + r - c + t_new)])


def _topk_select(scores):
    n = len(scores)
    sel = []
    for a in range(n):
        cnt = jnp.zeros(scores[a].shape, F32)
        for b in range(n):
            if b == a:
                continue
            ahead = (scores[b] >= scores[a]) if b < a else (scores[b] > scores[a])
            cnt = cnt + jnp.where(ahead, 1.0, 0.0)
        keep = cnt < MOBA_TOPK
        sel.append(keep)
    return sel


def _moba_p_body(tbl_ref, q_ref, k_ref, v_ref, bias_ref, o_ref, km_sc, vt_sc, lg_sc, *, nblk, dh, hps):
    hg = pl.program_id(1)
    j = pl.program_id(2)
    blk = MOBA_BLOCK

    @pl.when(j == 0)
    def _():
        km_sc[...] = jnp.zeros_like(km_sc)
        for hh in range(hps):
            sl = slice(hh * dh, (hh + 1) * dh)
            for b in range(nblk):
                kblk = k_ref[b * blk:(b + 1) * blk, sl].astype(F32)
                km_sc[hh, b:b + 1, :] = jnp.mean(kblk, axis=0, keepdims=True)
            vt_sc[hh] = v_ref[:, sl].astype(F32).T.astype(BF16)

    def select(hh):
        qs = q_ref[:, hh * dh:(hh + 1) * dh]
        st = _dot_nt(km_sc[hh].astype(BF16), qs)
        rowi = lax.broadcasted_iota(jnp.int32, st.shape, 0)
        st = jnp.where(rowi < j, st, -jnp.inf)
        cnt = jnp.zeros(st.shape, F32)
        for b in range(nblk):
            sb = st[b:b + 1, :]
            cnt = cnt + jnp.where(rowi > b, jnp.where(sb >= st, 1.0, 0.0), jnp.where(sb > st, 1.0, 0.0))
        return qs, jnp.where(cnt < MOBA_TOPK, jnp.where(rowi < j, 1.0, 0.0), 0.0)

    picked = [select(hh) for hh in range(hps)]

    def run(jj):
        for hh in range(hps):
            sl = slice(hh * dh, (hh + 1) * dh)
            qs, sel_t = picked[hh]
            far = tbl_ref[hg * hps + hh, REL_BUCKETS - 1]
            m = None
            for b in range(jj + 1):
                lg = _dot_nt(k_ref[b * blk:(b + 1) * blk, sl], qs)
                if b == jj:
                    lg = lg + bias_ref[hh, 0]
                elif b == jj - 1:
                    lg = lg + bias_ref[hh, 1]
                else:
                    lg = lg + far
                if b < jj:
                    lg = jnp.where(sel_t[b:b + 1, :] > 0.5, lg, NEG)
                lg_sc[hh, b] = lg
                tmax = jnp.max(lg, axis=0, keepdims=True)
                m = tmax if m is None else jnp.maximum(m, tmax)
            l = jnp.zeros_like(m)
            acc = jnp.zeros((dh, blk), F32)
            for b in range(jj + 1):
                pr = jnp.exp(lg_sc[hh, b] - m)
                l = l + jnp.sum(pr, axis=0, keepdims=True)
                acc = acc + _dot(vt_sc[hh, :, b * blk:(b + 1) * blk], pr.astype(BF16))
            o_ref[:, sl] = (acc / l).T

    for jj in range(nblk):
        pl.when(j == jj)(functools.partial(run, jj))


def _moba_prompt(rel_t, p, bias, *, bsz, seq, colq, colk, colv, nhead, dh):
    assert dh == LANE and seq % MOBA_BLOCK == 0 and MOBA_BLOCK >= REL_MAX_DIST
    nblk = seq // MOBA_BLOCK
    assert nblk <= SUBLANE
    ntile = bias.shape[1]
    hps = 2
    assert nhead % hps == 0 and colq % hps == 0 and colk % hps == 0 and colv % hps == 0
    cq, ck, cv = colq // hps, colk // hps, colv // hps
    return pl.pallas_call(
        functools.partial(_moba_p_body, nblk=nblk, dh=dh, hps=hps),
        grid=(bsz, nhead // hps, nblk),
        in_specs=[
            pl.BlockSpec(memory_space=pltpu.SMEM),
            pl.BlockSpec((MOBA_BLOCK, hps * dh), lambda b, h, j: (b * nblk + j, cq + h)),
            pl.BlockSpec((seq, hps * dh), lambda b, h, j: (b, ck + h)),
            pl.BlockSpec((seq, hps * dh), lambda b, h, j: (b, cv + h)),
            pl.BlockSpec((hps, ntile, MOBA_BLOCK, MOBA_BLOCK), lambda b, h, j: (h, 0, 0, 0)),
        ],
        out_specs=pl.BlockSpec((MOBA_BLOCK, hps * dh), lambda b, h, j: (b * nblk + j, h)),
        out_shape=jax.ShapeDtypeStruct((bsz * seq, nhead * dh), F32),
        scratch_shapes=[
            pltpu.VMEM((hps, 2 * SUBLANE, dh), F32),
            pltpu.VMEM((hps, dh, seq), BF16),
            pltpu.VMEM((hps, nblk, MOBA_BLOCK, MOBA_BLOCK), F32),
        ],
        compiler_params=_cparams(("parallel", "parallel", "arbitrary")),
    )(rel_t, p, p, p, bias)


def _moba_s_body(pt_ref, q_ref, kn_ref, vn_ref, bias_ref, *rest, npage, nhead, dh, t_new, page, bsz):
    kp = rest[:npage]
    vp = rest[npage:2 * npage]
    o_ref, p_sc, l_sc, acc_sc = rest[2 * npage:]
    s = pl.program_id(0)
    ppb = MOBA_BLOCK // page
    nblk = npage // ppb

    def heads(x):
        return jnp.stack([x[:, h * dh:(h + 1) * dh] for h in range(nhead)], axis=0)

    def pad_rows(x):
        return jnp.concatenate([x, jnp.zeros((page - t_new, x.shape[1]), x.dtype)], axis=0)

    @pl.when(s >= 1)
    def _():
        slot = (s + 1) % 2
        acc = acc_sc[slot]
        for pg in range(npage):
            acc = acc + jnp.einsum('htk,hkd->htd', p_sc[slot, pg].astype(BF16), vp[pg][0, 0].astype(BF16),
                                   preferred_element_type=F32)
        out = acc / l_sc[slot]
        for h in range(nhead):
            o_ref[:, h * dh:(h + 1) * dh] = out[h]

    @pl.when(s < bsz)
    def _():
        slot = s % 2
        q3 = heads(q_ref[...]).astype(BF16)
        lgs =[jnp.einsum('htd,hkd->htk', q3, kp[pg][0, 0].astype(BF16), preferred_element_type=F32)
               for pg in range(npage)]
        scores = []
        for b in range(nblk):
            tot = lgs[b * ppb]
            for r in range(1, ppb):
                tot = tot + lgs[b * ppb + r]
            scores.append(jnp.sum(tot, axis=-1, keepdims=True))
        sel = _topk_select(scores)
        far = bias_ref[:, 2]
        prev = bias_ref[:, 0]
        ml = []
        for pg in range(npage):
            b = pg // ppb
            r = pg % ppb
            bias = prev[:, :, r * page:(r + 1) * page] if b == nblk - 1 else far[:, :, :page]
            ml.append(jnp.where(sel[b], lgs[pg] + bias, NEG))
        kn3 = heads(pad_rows(kn_ref[...])).astype(BF16)
        vn3 = heads(pad_rows(vn_ref[...])).astype(BF16)
        lo = jnp.einsum('htd,hkd->htk', q3, kn3, preferred_element_type=F32) + bias_ref[:, 1][:, :, :page]
        m = jnp.max(lo, axis=-1, keepdims=True)
        for x in ml:
            m = jnp.maximum(m, jnp.max(x, axis=-1, keepdims=True))
        eo = jnp.exp(lo - m)
        l = jnp.sum(eo, axis=-1, keepdims=True)
        for pg in range(npage):
            e = jnp.exp(ml[pg] - m)
            l = l + jnp.sum(e, axis=-1, keepdims=True)
            p_sc[slot, pg] = e
        l_sc[slot] = l
        acc_sc[slot] = jnp.einsum('htk,hkd->htd', eo.astype(BF16), vn3, preferred_element_type=F32)


def _moba_sample(p, bias, kpool, vpool, layer, page_table, *, row_off, bsz, t_new, colq, colk, colv, nhead, dh):
    npage = page_table.shape[1]
    page = kpool.shape[3]
    assert dh == LANE and page == LANE and MOBA_BLOCK % page == 0
    assert (npage * page) % MOBA_BLOCK == 0 and npage * page >= MOBA_BLOCK and t_new <= page
    w = nhead * dh
    rb0 = row_off // t_new
    ntile = bias.shape[1]
    pt = page_table.reshape(-1).astype(jnp.int32)
    assert colq % nhead == 0 and colk % nhead == 0 and colv % nhead == 0
    colq, colk, colv = colq // nhead, colk // nhead, colv // nhead

    def cur(s):
        return jnp.minimum(s, bsz - 1)

    def prev(s):
        return jnp.maximum(s - 1, 0)

    def kmap(pg):
        return lambda s, pt_ref: (layer, pt_ref[cur(s) * npage + pg], 0, 0, 0)

    def vmap_(pg):
        return lambda s, pt_ref: (layer, pt_ref[prev(s) * npage + pg], 0, 0, 0)

    in_specs = [
        pl.BlockSpec((t_new, w), lambda s, pt_ref: (rb0 + cur(s), colq)),
        pl.BlockSpec((t_new, w), lambda s, pt_ref: (rb0 + cur(s), colk)),
        pl.BlockSpec((t_new, w), lambda s, pt_ref: (rb0 + cur(s), colv)),
        pl.BlockSpec((nhead, ntile, t_new, MOBA_BLOCK), lambda s, pt_ref: (0, 0, 0, 0)),
    ]
    in_specs += [pl.BlockSpec((1, 1, nhead, page, dh), kmap(pg)) for pg in range(npage)]
    in_specs += [pl.BlockSpec((1, 1, nhead, page, dh), vmap_(pg)) for pg in range(npage)]
    grid_spec = pltpu.PrefetchScalarGridSpec(
        num_scalar_prefetch=1,
        grid=(bsz + 1,),
        in_specs=in_specs,
        out_specs=pl.BlockSpec((t_new, w), lambda s, pt_ref: (prev(s), 0)),
        scratch_shapes=[
            pltpu.VMEM((2, npage, nhead, t_new, page), F32),
            pltpu.VMEM((2, nhead, t_new, 1), F32),
            pltpu.VMEM((2, nhead, t_new, dh), F32),
        ],
    )
    return pl.pallas_call(
        functools.partial(_moba_s_body, npage=npage, nhead=nhead, dh=dh, t_new=t_new, page=page, bsz=bsz),
        grid_spec=grid_spec,
        out_shape=jax.ShapeDtypeStruct((bsz * t_new, w), F32),
        compiler_params=_cparams(("arbitrary",)),
    )(pt, p, p, p, bias, *([kpool] * npage), *([vpool] * npage))


def _mix_body(*refs, starts):
    ns = len(starts) - 1
    hm_refs, hb_refs = refs[:ns], refs[ns:2 * ns]
    ga_ref, gb_ref, x_ref, wa_ref, wb_ref, wo_ref, o_ref = refs[2 * ns:]
    i = pl.program_id(0)
    a = _dot(_seg_load(hm_refs, starts, i).astype(BF16), wa_ref[...])
    b = _dot(_seg_load(hb_refs, starts, i).astype(BF16), wb_ref[...])
    merged = jax.nn.sigmoid(ga_ref[...].astype(F32)) * a + jax.nn.sigmoid(gb_ref[...].astype(F32)) * b
    o_ref[...] = x_ref[...] + _dot(merged.astype(BF16), wo_ref[...])


def _mix(hms, hbs, p, x, wa, wb, wo):
    m, d = x.shape
    wm = hms[0].shape[1]
    wbw = hbs[0].shape[1]
    rows = [h.shape[0] for h in hms]
    assert rows == [h.shape[0] for h in hbs] and sum(rows) == m
    tm = _pick(math.gcd(*rows), (256, 128, 64, 32, 16, 8))
    starts = _seg_starts(rows, tm)
    const = dict(pipeline_mode=pl.Buffered(1))
    return pl.pallas_call(
        functools.partial(_mix_body, starts=starts),
        grid=(m // tm,),
        in_specs=[_seg_spec(starts, k, tm, wm) for k in range(len(rows))]
        + [_seg_spec(starts, k, tm, wbw) for k in range(len(rows))]
        + [
            pl.BlockSpec((tm, d), lambda i: (i, 0)),
            pl.BlockSpec((tm, d), lambda i: (i, 1)),
            pl.BlockSpec((tm, d), lambda i: (i, 0)),
            pl.BlockSpec((wm, d), lambda i: (0, 0), **const),
            pl.BlockSpec((wbw, d), lambda i: (0, 0), **const),
            pl.BlockSpec((d, d), lambda i: (0, 0), **const),
        ],
        out_specs=pl.BlockSpec((tm, d), lambda i: (i, 0)),
        out_shape=jax.ShapeDtypeStruct((m, d), F32),
        compiler_params=_cparams(("parallel",)),
    )(*hms, *hbs, p, p, x, wa, wb, wo)


def kernel(x_prompt, x_sample, cache_k, cache_v, state_C, state_n, state_m, page_table, g_ff1, w_ff1_gate,
           w_ff1_up, w_ff1_down, g_mix, w_in, b_ig, b_fg, g_head, w_a, w_b, w_out, g_ff2, w_ff2_gate,
           w_ff2_up, w_ff2_down, rel_bias_table, g_final):
    bp, seq, d = x_prompt.shape
    bd, t_new, _ = x_sample.shape
    depth = g_ff1.shape[0]
    nh_m = b_ig.shape[1]
    w_m = w_a.shape[1]
    dh_m = w_m // nh_m
    nh_b = rel_bias_table.shape[1]
    w_bw = w_b.shape[1]
    dh_b = w_bw // nh_b
    page = cache_k.shape[3]
    mp = bp * seq
    sdt = state_C.dtype
    assert 2 * nh_m <= SUBLANE and (2 * d) % w_m == 0 and (2 * d + 4 * w_m) % LANE == 0
    assert mp % t_new == 0 and seq % MLSTM_PROMPT_CHUNK == 0 and seq % page == 0

    o_gate = 4 * w_m
    o_moba = o_gate + 2 * nh_m
    o_ga = o_moba + 3 * w_bw
    col_m = (2 * d) // w_m
    col_b = (2 * d + 4 * w_m) // dh_b
    colq_b, colk_b, colv_b = col_b, col_b + nh_b, col_b + 2 * nh_b

    n_slab = 2 * d + 4 * w_m + 3 * w_bw
    col_scale = np.ones((1, n_slab), np.float32)
    col_scale[:, 2 * d + w_m:2 * d + 2 * w_m] = dh_m ** -0.5
    col_scale[:, 2 * d + 4 * w_m:2 * d + 4 * w_m + w_bw] = dh_b ** -0.5
    col_scale = jnp.asarray(col_scale)

    rel_t = rel_bias_table.T.astype(F32)
    bias_p = _bias_tiles(rel_t, _prompt_bias_maps())
    bias_s = _bias_tiles(rel_t, _sample_bias_maps(t_new))

    ms_rows = bd * t_new
    xs = [x_prompt.reshape(mp, d), x_sample.reshape(ms_rows, d)]
    caches = (jnp.zeros((depth, mp // page, nh_b, page, dh_b), F32), jnp.zeros((depth, mp // page, nh_b, page, dh_b), F32),
              jnp.zeros((depth, bd, nh_b, t_new, dh_b), F32), jnp.zeros((depth, bd, nh_b, t_new, dh_b), F32))
    c_prompt = jnp.zeros((depth, bp, nh_m, dh_m, dh_m), F32)
    c_sample = jnp.zeros((depth, bd, nh_m, dh_m, dh_m), F32)
    np_l, mp_l, ns_l, ms_l = [], [], [], []
    for l in range(depth):
        wi = w_in[l]
        w_main = jnp.concatenate([wi[:, o_ga:], wi[:, :o_gate], wi[:, o_moba:o_ga]], axis=1).astype(BF16)
        w_gate = jnp.pad(wi[:, o_gate:o_moba], ((0, 0), (0, LANE - 2 * nh_m))).astype(BF16)
        gate_bias = jnp.concatenate([b_ig[l], b_fg[l]]).astype(F32)
        bias_c = jnp.pad(gate_bias, (0, LANE - 2 * nh_m)).reshape(1, LANE)
        bias_r = jnp.pad(gate_bias, (0, SUBLANE - 2 * nh_m)).reshape(SUBLANE, 1)

        x, hn = _ffn(xs, g_ff1[l], _cast_layer(w_ff1_gate, l), _cast_layer(w_ff1_up, l), _cast_layer(w_ff1_down, l),
                     g_next=g_mix[l])
        p, gates, gt_p, caches = _proj(hn, w_main, w_gate, col_scale, caches, l, mp=mp, nhead=nh_b,
                                       dh=dh_b, page=page, t_new=t_new)
        p_s = p[mp:].astype(F32)

        gt_s = gt_p[:, mp:].reshape(SUBLANE, bd, t_new)
        gt_s = jnp.pad(gt_s, ((0, 0), (0, 0), (0, LANE - t_new))).reshape(SUBLANE, bd * LANE)

        mk = dict(col0=col_m, nh=nh_m, dh=dh_m)
        hm_p, (c_prompt, npp, mpp) = _mlstm(p, gates, gt_p, bias_c, bias_r, g_head[l], c_prompt, l, row_off=0,
                                            bsz=bp, seq=seq, chunk=MLSTM_PROMPT_CHUNK, **mk)
        init = (state_C, state_n, state_m)
        hm_s, (c_sample, ns, ms) = _mlstm(p_s, gates[mp:], gt_s, bias_c, bias_r, g_head[l], c_sample, l, row_off=0,
                                          bsz=bd, seq=t_new, chunk=t_new, init=init, ne=math.gcd(bd, 4), **mk)

        bk = dict(colq=colq_b, colk=colk_b, colv=colv_b, nhead=nh_b, dh=dh_b)
        hb_p = _moba_prompt(rel_t, p, bias_p, bsz=bp, seq=seq, **bk)
        hb_s = _moba_sample(p_s, bias_s, cache_k, cache_v, l, page_table, row_off=0, bsz=bd, t_new=t_new, **bk)

        x = _mix([hm_p, hm_s], [hb_p, hb_s], p, x, _cast_layer(w_a, l), _cast_layer(w_b, l), _cast_layer(w_out, l))
        ffn2 = (g_ff2[l], _cast_layer(w_ff2_gate, l), _cast_layer(w_ff2_up, l), _cast_layer(w_ff2_down, l))
        if l == depth - 1:
            y_prompt, y_sample = _ffn([x], *ffn2, g_final=g_final, out_rows=(mp, ms_rows))
        else:
            xs = [_ffn([x], *ffn2)]

        np_l.append(npp.astype(sdt))
        mp_l.append(mpp.astype(sdt))
        ns_l.append(ns.astype(sdt))
        ms_l.append(ms.astype(sdt))

    k_pages, v_pages, k_rows, v_rows = caches
    page_shape = (depth, bp, seq // page, nh_b, page, dh_b)
    return (y_prompt.reshape(bp, seq, d), y_sample.reshape(bd, t_new, d),
            k_pages.reshape(page_shape), v_pages.reshape(page_shape), k_rows, v_rows,
            c_prompt.astype(sdt), jnp.stack(np_l), jnp.stack(mp_l), c_sample.astype(sdt), jnp.stack(ns_l), jnp.stack(ms_l))
```

```python
import functools
import math

import numpy as np
import jax
import jax.numpy as jnp
from jax import lax
from jax.experimental import pallas as pl
from jax.experimental.pallas import tpu as pltpu

F32 = jnp.float32
BF16 = jnp.bfloat16

NORM_EPS = 1e-6
MOBA_BLOCK = 256
MOBA_TOPK = 3
REL_BUCKETS = 32
REL_MAX_DIST = 128
MLSTM_PROMPT_CHUNK = 256
LANE = 128
SUBLANE = 8
NEG = -1e30
VMEM_LIMIT = 56 * 1024 * 1024


def _pick(n, cands):
    for c in cands:
        if n % c == 0:
            return c
    return n


def _cparams(sem):
    return pltpu.CompilerParams(dimension_semantics=sem, vmem_limit_bytes=VMEM_LIMIT)


def _rms(x, g):
    return x * lax.rsqrt(jnp.mean(x * x, axis=-1, keepdims=True) + NORM_EPS) * g


def _dot(a, b):
    return jnp.dot(a, b, preferred_element_type=F32)


def _dot_nt(a, b):
    return lax.dot_general(a, b, (((1,), (1,)), ((), ())), preferred_element_type=F32)


def _dot_tn(a, b):
    return lax.dot_general(a, b, (((0,), (0,)), ((), ())), preferred_element_type=F32)


CAST_BLOCK_BYTES = 6 * 1024 * 1024


def _cast_body(w_ref, o_ref):
    o_ref[...] = w_ref[0].astype(o_ref.dtype)


def _cast_layer(w, layer):
    _, rows, cols = w.shape
    tr = rows
    while tr % 2 == 0 and tr // 2 >= 2 * SUBLANE and tr * cols * 4 > CAST_BLOCK_BYTES:
        tr //= 2
    return pl.pallas_call(
        _cast_body,
        grid=(rows // tr,),
        in_specs=[pl.BlockSpec((1, tr, cols), lambda i: (layer, i, 0))],
        out_specs=pl.BlockSpec((tr, cols), lambda i: (i, 0)),
        out_shape=jax.ShapeDtypeStruct((rows, cols), BF16),
        compiler_params=_cparams(("parallel",)),
    )(w)


def _seg_starts(rows, tm):
    starts = [0]
    for r in rows:
        assert r % tm == 0
        starts.append(starts[-1] + r // tm)
    return starts


def _seg_spec(starts, k, tm, width, col=0, ngrid=1, **kw):
    lo, n = starts[k], starts[k + 1] - starts[k]
    if ngrid == 1:
        return pl.BlockSpec((tm, width), lambda i: (jnp.clip(i - lo, 0, n - 1), col), **kw)
    return pl.BlockSpec((tm, width), lambda i, j: (jnp.clip(i - lo, 0, n - 1), col), **kw)


def _seg_load(refs, starts, i):
    x = refs[0][...]
    for k in range(1, len(refs)):
        x = jnp.where(i >= starts[k], refs[k][...], x)
    return x


def _ffn_body(*refs, in_starts, out_starts, final, has_next):
    nx, no = len(in_starts) - 1, len(out_starts) - 1
    x_refs = refs[:nx]
    g_ref, wg_ref, wu_ref, wd_ref = refs[nx:nx + 4]
    rest = refs[nx + 4:]
    if final:
        gf_ref, rest = rest[0], rest[1:]
    if has_next:
        gn_ref, rest = rest[0], rest[1:]
    o_refs, rest = rest[:no], rest[no:]
    if has_next:
        hnext_ref, rest = rest[0], rest[1:]
    hn_ref = rest[0]
    acc_ref = rest[1] if no > 1 else o_refs[0]
    i = pl.program_id(0)
    f = pl.program_id(1)

    @pl.when(f == 0)
    def _():
        hn_ref[...] = _rms(_seg_load(x_refs, in_starts, i), g_ref[...]).astype(BF16)
        acc_ref[...] = jnp.zeros_like(acc_ref)

    h = hn_ref[...]
    g = _dot(h, wg_ref[...])
    u = _dot(h, wu_ref[...])
    a = (g * jax.nn.sigmoid(g) * u).astype(BF16)
    acc_ref[...] += _dot(a, wd_ref[...])

    @pl.when(f == pl.num_programs(1) - 1)
    def _():
        y = _seg_load(x_refs, in_starts, i) + 0.5 * acc_ref[...]
        if final:
            y = _rms(y, gf_ref[...])
        if has_next:
            hnext_ref[...] = _rms(y, gn_ref[...]).astype(BF16)
        if no == 1:
            o_refs[0][...] = y
        else:
            for k in range(no):
                def store(k=k):
                    o_refs[k][...] = y
                pl.when(jnp.logical_and(i >= out_starts[k], i < out_starts[k + 1]))(store)


def _ffn(xs, g, wg, wu, wd, g_final=None, out_rows=None, g_next=None):
    d = xs[0].shape[1]
    rows_in = [x.shape[0] for x in xs]
    m = sum(rows_in)
    rows_out = list(out_rows) if out_rows is not None else [m]
    ff = wg.shape[1]
    tm = _pick(math.gcd(*rows_in, *rows_out), (512, 256, 128, 64, 32, 16, 8))
    tf = _pick(ff, (512, 256, 128))
    final = g_final is not None
    in_starts = _seg_starts(rows_in, tm)
    out_starts = _seg_starts(rows_out, tm)
    in_specs = [_seg_spec(in_starts, k, tm, d, ngrid=2) for k in range(len(xs))]
    in_specs += [
        pl.BlockSpec((1, d), lambda i, f: (0, 0)),
        pl.BlockSpec((d, tf), lambda i, f: (0, f)),
        pl.BlockSpec((d, tf), lambda i, f: (0, f)),
        pl.BlockSpec((tf, d), lambda i, f: (f, 0)),
    ]
    args = [*xs, g.reshape(1, d), wg, wu, wd]
    if final:
        in_specs.append(pl.BlockSpec((1, d), lambda i, f: (0, 0)))
        args.append(g_final.reshape(1, d))
    has_next = g_next is not None
    out_specs = [_seg_spec(out_starts, k, tm, d, ngrid=2) for k in range(len(rows_out))]
    out_shape = [jax.ShapeDtypeStruct((r, d), F32) for r in rows_out]
    if has_next:
        assert out_rows is None
        in_specs.append(pl.BlockSpec((1, d), lambda i, f: (0, 0)))
        args.append(g_next.reshape(1, d))
        out_specs.append(pl.BlockSpec((tm, d), lambda i, f: (i, 0)))
        out_shape.append(jax.ShapeDtypeStruct((m, d), BF16))
    scratch = [pltpu.VMEM((tm, d), BF16)]
    if len(rows_out) > 1:
        scratch.append(pltpu.VMEM((tm, d), F32))
    outs = pl.pallas_call(
        functools.partial(_ffn_body, in_starts=in_starts, out_starts=out_starts, final=final, has_next=has_next),
        grid=(m // tm, ff // tf),
        in_specs=in_specs,
        out_specs=out_specs,
        out_shape=out_shape,
        scratch_shapes=scratch,
        compiler_params=_cparams(("parallel", "arbitrary")),
    )(*args)
    return outs if (out_rows is not None or has_next) else outs[0]


def _proj_body(hn_ref, w_ref, wgate_ref, cs_ref, *rest, j_k, j_v, npt, nhead, dh, page, t_new):
    p_ref, gate_ref, gatet_ref, kp_ref, vp_ref, ks_ref, vs_ref = rest[-7:]
    i = pl.program_id(0)
    j = pl.program_id(1)
    tm = hn_ref.shape[0]

    @pl.when(j == 0)
    def _():
        gate = _dot(hn_ref[...], wgate_ref[...])
        gate_ref[...] = gate
        gatet_ref[...] = gate.T[:SUBLANE, :]

    acc = _dot(hn_ref[...], w_ref[...])
    p_ref[...] = (acc * cs_ref[...]).astype(BF16)

    def put(pages_ref, rows_ref):
        @pl.when(i < npt)
        def _():
            for pg in range(tm // page):
                for h in range(nhead):
                    pages_ref[0, pg, h] = acc[pg * page:(pg + 1) * page, h * dh:(h + 1) * dh]

        @pl.when(i >= npt)
        def _():
            for h in range(nhead):
                rows_ref[0, :, h] = acc[:, h * dh:(h + 1) * dh].reshape(tm // t_new, t_new, dh)

    pl.when(j == j_k)(functools.partial(put, kp_ref, ks_ref))
    pl.when(j == j_v)(functools.partial(put, vp_ref, vs_ref))


def _proj(hn, w_main, w_gate, col_scale, caches, layer, *, mp, nhead, dh, page, t_new):
    carried = [] if isinstance(caches[0], jax.ShapeDtypeStruct) else list(caches)
    m, d = hn.shape
    n = w_main.shape[1]
    w = nhead * dh
    tn = w
    tm = _pick(math.gcd(mp, m - mp), (1024, 512, 256, 128))
    assert n % tn == 0 and tm % page == 0 and tm % t_new == 0
    j_k = (n - 2 * w) // tn
    npt = mp // tm
    nst = (m - mp) // tm
    once = dict(pipeline_mode=pl.Buffered(1))
    alias = pl.BlockSpec(memory_space=pl.ANY)
    pages_spec = pl.BlockSpec((1, tm // page, nhead, page, dh),
                              lambda i, j: (layer, jnp.minimum(i, npt - 1), 0, 0, 0), **once)
    rows_spec = pl.BlockSpec((1, tm // t_new, nhead, t_new, dh),
                             lambda i, j: (layer, jnp.clip(i - npt, 0, nst - 1), 0, 0, 0), **once)
    outs = pl.pallas_call(
        functools.partial(_proj_body, j_k=j_k, j_v=j_k + 1, npt=npt, nhead=nhead, dh=dh, page=page, t_new=t_new),
        grid=(m // tm, n // tn),
        in_specs=[
            pl.BlockSpec((tm, d), lambda i, j: (i, 0)),
            pl.BlockSpec((d, tn), lambda i, j: (0, j)),
            pl.BlockSpec((d, LANE), lambda i, j: (0, 0)),
            pl.BlockSpec((1, tn), lambda i, j: (0, j)),
        ] + [alias] * len(carried),
        out_specs=[
            pl.BlockSpec((tm, tn), lambda i, j: (i, j)),
            pl.BlockSpec((tm, LANE), lambda i, j: (i, 0)),
            pl.BlockSpec((SUBLANE, tm), lambda i, j: (0, i)),
            pages_spec, pages_spec, rows_spec, rows_spec,
        ],
        out_shape=[jax.ShapeDtypeStruct((m, n), BF16), jax.ShapeDtypeStruct((m, LANE), F32),
                   jax.ShapeDtypeStruct((SUBLANE, m), F32)]
        + [jax.ShapeDtypeStruct(c.shape, c.dtype) for c in caches],
        input_output_aliases={4 + k: 3 + k for k in range(len(carried))},
        compiler_params=_cparams(("parallel", "arbitrary")),
    )(hn, w_main, w_gate, col_scale, *carried)
    return outs[0], outs[1], outs[2], tuple(outs[3:])


def _log_sigmoid(x):
    return jnp.minimum(x, 0.0) - jnp.log(1.0 + jnp.exp(-jnp.abs(x)))


def _mlstm_body(*refs, nh, dh, lq, lk, ne, has_init):
    q_ref, k_ref, v_ref, o_ref, gc_ref, gr_ref, bc_ref, br_ref, gh_ref = refs[:9]
    if has_init:
        c0_ref, n0_ref, m0_ref = refs[9:12]
    h_ref, cn_ref, nn_ref, mn_ref, c_sc, n_sc, m_sc = refs[-7:]
    c = pl.program_id(1)
    last = pl.num_programs(1) - 1

    @pl.when(c == 0)
    def _():
        if has_init:
            c_sc[...] = c0_ref[0].astype(F32)
            n_sc[...] = n0_ref[0].astype(F32)
            m_sc[...] = m0_ref[0].astype(F32)
        else:
            c_sc[...] = jnp.zeros_like(c_sc)
            n_sc[...] = jnp.zeros_like(n_sc)
            m_sc[...] = jnp.zeros_like(m_sc)

    for e in range(ne):
        rows = slice(e * lq, (e + 1) * lq)
        gcol = gc_ref[rows, :] + bc_ref[...]
        grow = gr_ref[:, e * lk:(e + 1) * lk] + br_ref[...]
        lf_col = _log_sigmoid(gcol)
        lf_row = _log_sigmoid(grow)
        if lq >= LANE:
            ri = lax.broadcasted_iota(jnp.int32, (lq, lq), 0)
            ci = lax.broadcasted_iota(jnp.int32, (lq, lq), 1)
            tril = jnp.where(ci <= ri, 1.0, 0.0).astype(F32)
            triu = jnp.where(ri <= ci, 1.0, 0.0).astype(F32)
            b_col = jnp.dot(tril, lf_col, precision=lax.Precision.HIGHEST, preferred_element_type=F32)
            b_row = jnp.dot(lf_row, triu, precision=lax.Precision.HIGHEST, preferred_element_type=F32)
        else:
            ri = lax.broadcasted_iota(jnp.int32, (lq, LANE), 0)
            ci = lax.broadcasted_iota(jnp.int32, (SUBLANE, lk), 1)
            b_col = jnp.zeros((lq, LANE), F32)
            b_row = jnp.zeros((SUBLANE, lk), F32)
            for s in range(lq):
                b_col = b_col + jnp.where(ri >= s, lf_col[s:s + 1, :], 0.0)
                b_row = b_row + jnp.where(ci >= s, lf_row[:, s:s + 1], 0.0)

        qi = lax.broadcasted_iota(jnp.int32, (lq, lk), 0)
        ki = lax.broadcasted_iota(jnp.int32, (lq, lk), 1)
        causal = ki <= qi
        kvalid = lax.broadcasted_iota(jnp.int32, (lk, 1), 0) < lq
        q_all = q_ref[rows, :]
        k_all = k_ref[rows, :]
        v_all = v_ref[rows, :]
        o_all = o_ref[rows, :]
        gh_all = gh_ref[...]
        m_prev_all = m_sc[e]

        def pad_keys(a):
            if lk == lq:
                return a
            return jnp.concatenate([a, jnp.zeros((lk - lq, a.shape[1]), a.dtype)], axis=0)

        for h in range(nh):
            sl = slice(h * dh, (h + 1) * dh)
            qh = q_all[:, sl].astype(F32)
            kh = pad_keys(k_all[:, sl]).astype(F32)
            vh = pad_keys(v_all[:, sl]).astype(F32)
            bq = b_col[:, nh + h:nh + h + 1]
            ig_k = pad_keys(gcol[:, h:h + 1])
            bk = pad_keys(bq)
            r_row = grow[h:h + 1, :] - b_row[nh + h:nh + h + 1, :]
            m_prev = m_prev_all[:, h:h + 1]
            log_d = jnp.where(causal, bq + r_row, -jnp.inf)
            log_p = bq + m_prev
            m_t = jnp.maximum(log_p, jnp.max(log_d, axis=-1, keepdims=True))
            w_intra = jnp.exp(log_d - m_t)
            w_prev = jnp.exp(log_p - m_t)
            qb = qh.astype(BF16)
            kb = kh.astype(BF16)
            s = _dot_nt(qb, kb) * w_intra
            cmat = c_sc[e, h]
            nrow = n_sc[e, h:h + 1, :]
            num = _dot(s.astype(BF16), vh.astype(BF16)) + w_prev * _dot_nt(qb, cmat.astype(BF16))
            den = jnp.sum(s, axis=-1, keepdims=True) + w_prev * jnp.sum(qh * nrow, axis=-1, keepdims=True)
            hh = num / jnp.maximum(jnp.abs(den), jnp.exp(-m_t))
            b_last = bq[lq - 1:lq, :]
            m_last = m_t[lq - 1:lq, :]
            w_last = jnp.where(kvalid, jnp.exp(b_last - bk + ig_k - m_last), 0.0)
            decay = w_prev[lq - 1:lq, :]
            c_new = decay * cmat + _dot_tn((vh * w_last).astype(BF16), kb)
            n_new = decay * nrow + jnp.sum(kh * w_last, axis=0, keepdims=True)
            c_sc[e, h] = c_new
            n_sc[e, h:h + 1, :] = n_new
            m_sc[e, :, h:h + 1] = m_last
            hm = jax.nn.sigmoid(o_all[:, sl].astype(F32)) * hh
            h_ref[rows, sl] = _rms(hm, gh_all[:, sl])

    @pl.when(c == last)
    def _():
        cn_ref[0] = c_sc[...]
        nn_ref[...] = n_sc[...]
        mn_ref[...] = m_sc[...]


def _mlstm(p, gates, gates_t, bias_c, bias_r, g_head, c_stack, layer, *, row_off, bsz, seq, chunk, col0, nh, dh,
           init=None, ne=1):
    w = nh * dh
    lq = chunk
    lk = max(chunk, LANE)
    nc = seq // chunk
    assert ne == 1 or (nc == 1 and bsz % ne == 0)
    rb0 = row_off // (ne * lq)
    has_init = init is not None

    def rowblk(b, c):
        return rb0 + b * nc + c

    in_specs = [
        pl.BlockSpec((ne * lq, w), lambda b, c: (rowblk(b, c), col0)),
        pl.BlockSpec((ne * lq, w), lambda b, c: (rowblk(b, c), col0 + 1)),
        pl.BlockSpec((ne * lq, w), lambda b, c: (rowblk(b, c), col0 + 2)),
        pl.BlockSpec((ne * lq, w), lambda b, c: (rowblk(b, c), col0 + 3)),
        pl.BlockSpec((ne * lq, LANE), lambda b, c: (rowblk(b, c), 0)),
        pl.BlockSpec((SUBLANE, ne * lk), lambda b, c: (0, rowblk(b, c))),
        pl.BlockSpec((1, LANE), lambda b, c: (0, 0)),
        pl.BlockSpec((SUBLANE, 1), lambda b, c: (0, 0)),
        pl.BlockSpec((1, w), lambda b, c: (0, 0)),
    ]
    args = [p, p, p, p, gates, gates_t, bias_c, bias_r, g_head.reshape(1, w)]
    if has_init:
        c0, n0, m0 = init
        in_specs += [
            pl.BlockSpec((1, ne, nh, dh, dh), lambda b, c: (layer, b, 0, 0, 0)),
            pl.BlockSpec((1, ne, nh, dh), lambda b, c: (layer, b, 0, 0)),
            pl.BlockSpec((1, ne, 1, nh), lambda b, c: (layer, b, 0, 0)),
        ]
        args += [c0, n0, m0.reshape(m0.shape[0], bsz, 1, nh)]
    carried = not isinstance(c_stack, jax.ShapeDtypeStruct)
    if carried:
        in_specs.append(pl.BlockSpec(memory_space=pl.ANY))
        args.append(c_stack)
    out_specs = [
        pl.BlockSpec((ne * lq, w), lambda b, c: (b * nc + c, 0)),
        pl.BlockSpec((1, ne, nh, dh, dh), lambda b, c: (layer, b, 0, 0, 0)),
        pl.BlockSpec((ne, nh, dh), lambda b, c: (b, 0, 0)),
        pl.BlockSpec((ne, 1, nh), lambda b, c: (b, 0, 0)),
    ]
    out_shape = [
        jax.ShapeDtypeStruct((bsz * seq, w), F32),
        jax.ShapeDtypeStruct(c_stack.shape, F32),
        jax.ShapeDtypeStruct((bsz, nh, dh), F32),
        jax.ShapeDtypeStruct((bsz, 1, nh), F32),
    ]
    h, cn, nn, mn = pl.pallas_call(
        functools.partial(_mlstm_body, nh=nh, dh=dh, lq=lq, lk=lk, ne=ne, has_init=has_init),
        grid=(bsz // ne, nc),
        in_specs=in_specs,
        out_specs=out_specs,
        out_shape=out_shape,
        input_output_aliases={len(args) - 1: 1} if carried else {},
        scratch_shapes=[pltpu.VMEM((ne, nh, dh, dh), F32), pltpu.VMEM((ne, nh, dh), F32),
                        pltpu.VMEM((ne, 1, nh), F32)],
        compiler_params=_cparams(("parallel", "arbitrary")),
    )(*args)
    return h, (cn, nn, mn.reshape(bsz, nh))


def _bucket_np(dist):
    n = np.maximum(dist, 0)
    max_exact = REL_BUCKETS // 2
    nf = np.maximum(n, 1).astype(np.float32)
    large = max_exact + (np.log(nf / np.float32(max_exact)) / np.float32(math.log(REL_MAX_DIST / max_exact))
                         * np.float32(REL_BUCKETS - max_exact)).astype(np.int32)
    large = np.minimum(large, REL_BUCKETS - 1)
    return np.where(dist < 0, -1, np.where(n < max_exact, n, large)).astype(np.int32)


def _bias_body(tbl_ref, map_ref, o_ref, *, ntile):
    h = pl.program_id(0)
    for t in range(ntile):
        bm = map_ref[t]
        acc = jnp.full(bm.shape, NEG, F32)
        for b in range(REL_BUCKETS):
            acc = jnp.where(bm == b, tbl_ref[h, b], acc)
        o_ref[0, t] = acc


def _bias_tiles(rel_t, maps):
    nhead = rel_t.shape[0]
    ntile, r, c = maps.shape
    return pl.pallas_call(
        functools.partial(_bias_body, ntile=ntile),
        grid=(nhead,),
        in_specs=[
            pl.BlockSpec(memory_space=pltpu.SMEM),
            pl.BlockSpec((ntile, r, c), lambda h: (0, 0, 0)),
        ],
        out_specs=pl.BlockSpec((1, ntile, r, c), lambda h: (h, 0, 0, 0)),
        out_shape=jax.ShapeDtypeStruct((nhead, ntile, r, c), F32),
        compiler_params=_cparams(("arbitrary",)),
    )(rel_t, jnp.asarray(maps))


def _prompt_bias_maps():
    k = np.arange(MOBA_BLOCK)[:, None]
    q = np.arange(MOBA_BLOCK)[None, :]
    return np.stack([_bucket_np(q - k), _bucket_np(MOBA_BLOCK + q - k)])


def _sample_bias_maps(t_new):
    r = np.arange(t_new)[:, None]
    c = np.arange(MOBA_BLOCK)[None, :]
    own = np.where(c < t_new, r - c, -1)
    return np.stack([_bucket_np(MOBA_BLOCK + r - c), _bucket_np(own), _bucket_np(2 * MOBA_BLOCK + r - c + t_new)])


def _topk_select(scores):
    n = len(scores)
    sel = []
    for a in range(n):
        cnt = jnp.zeros(scores[a].shape, F32)
        for b in range(n):
            if b == a:
                continue
            ahead = (scores[b] >= scores[a]) if b < a else (scores[b] > scores[a])
            cnt = cnt + jnp.where(ahead, 1.0, 0.0)
        keep = cnt < MOBA_TOPK
        sel.append(keep)
    return sel


def _moba_p_body(tbl_ref, q_ref, k_ref, v_ref, bias_ref, o_ref, km_sc, vt_sc, lg_sc, *, nblk, dh, hps):
    hg = pl.program_id(1)
    j = pl.program_id(2)
    blk = MOBA_BLOCK

    @pl.when(j == 0)
    def _():
        km_sc[...] = jnp.zeros_like(km_sc)
        for hh in range(hps):
            sl = slice(hh * dh, (hh + 1) * dh)
            for b in range(nblk):
                kblk = k_ref[b * blk:(b + 1) * blk, sl].astype(F32)
                km_sc[hh, b:b + 1, :] = jnp.mean(kblk, axis=0, keepdims=True)
            vt_sc[hh] = v_ref[:, sl].astype(F32).T.astype(BF16)

    def select(hh):
        qs = q_ref[:, hh * dh:(hh + 1) * dh]
        st = _dot_nt(km_sc[hh].astype(BF16), qs)
        rowi = lax.broadcasted_iota(jnp.int32, st.shape, 0)
        st = jnp.where(rowi < j, st, -jnp.inf)
        cnt = jnp.zeros(st.shape, F32)
        for b in range(nblk):
            sb = st[b:b + 1, :]
            cnt = cnt + jnp.where(rowi > b, jnp.where(sb >= st, 1.0, 0.0), jnp.where(sb > st, 1.0, 0.0))
        return qs, jnp.where(cnt < MOBA_TOPK, jnp.where(rowi < j, 1.0, 0.0), 0.0)

    picked = [select(hh) for hh in range(hps)]

    def run(jj):
        for hh in range(hps):
            sl = slice(hh * dh, (hh + 1) * dh)
            qs, sel_t = picked[hh]
            far = tbl_ref[hg * hps + hh, REL_BUCKETS - 1]
            m = None
            for b in range(jj + 1):
                lg = _dot_nt(k_ref[b * blk:(b + 1) * blk, sl], qs)
                if b == jj:
                    lg = lg + bias_ref[hh, 0]
                elif b == jj - 1:
                    lg = lg + bias_ref[hh, 1]
                else:
                    lg = lg + far
                if b < jj:
                    lg = jnp.where(sel_t[b:b + 1, :] > 0.5, lg, NEG)
                lg_sc[hh, b] = lg
                tmax = jnp.max(lg, axis=0, keepdims=True)
                m = tmax if m is None else jnp.maximum(m, tmax)
            l = jnp.zeros_like(m)
            acc = jnp.zeros((dh, blk), F32)
            for b in range(jj + 1):
                pr = jnp.exp(lg_sc[hh, b] - m)
                l = l + jnp.sum(pr, axis=0, keepdims=True)
                acc = acc + _dot(vt_sc[hh, :, b * blk:(b + 1) * blk], pr.astype(BF16))
            o_ref[:, sl] = (acc / l).T

    for jj in range(nblk):
        pl.when(j == jj)(functools.partial(run, jj))


def _moba_prompt(rel_t, p, bias, *, bsz, seq, colq, colk, colv, nhead, dh):
    assert dh == LANE and seq % MOBA_BLOCK == 0 and MOBA_BLOCK >= REL_MAX_DIST
    nblk = seq // MOBA_BLOCK
    assert nblk <= SUBLANE
    ntile = bias.shape[1]
    hps = 2
    assert nhead % hps == 0 and colq % hps == 0 and colk % hps == 0 and colv % hps == 0
    cq, ck, cv = colq // hps, colk // hps, colv // hps
    return pl.pallas_call(
        functools.partial(_moba_p_body, nblk=nblk, dh=dh, hps=hps),
        grid=(bsz, nhead // hps, nblk),
        in_specs=[
            pl.BlockSpec(memory_space=pltpu.SMEM),
            pl.BlockSpec((MOBA_BLOCK, hps * dh), lambda b, h, j: (b * nblk + j, cq + h)),
            pl.BlockSpec((seq, hps * dh), lambda b, h, j: (b, ck + h)),
            pl.BlockSpec((seq, hps * dh), lambda b, h, j: (b, cv + h)),
            pl.BlockSpec((hps, ntile, MOBA_BLOCK, MOBA_BLOCK), lambda b, h, j: (h, 0, 0, 0)),
        ],
        out_specs=pl.BlockSpec((MOBA_BLOCK, hps * dh), lambda b, h, j: (b * nblk + j, h)),
        out_shape=jax.ShapeDtypeStruct((bsz * seq, nhead * dh), F32),
        scratch_shapes=[
            pltpu.VMEM((hps, 2 * SUBLANE, dh), F32),
            pltpu.VMEM((hps, dh, seq), BF16),
            pltpu.VMEM((hps, nblk, MOBA_BLOCK, MOBA_BLOCK), F32),
        ],
        compiler_params=_cparams(("parallel", "parallel", "arbitrary")),
    )(rel_t, p, p, p, bias)


def _moba_s_body(pt_ref, q_ref, kn_ref, vn_ref, bias_ref, *rest, npage, nhead, dh, t_new, page, bsz):
    kp = rest[:npage]
    vp = rest[npage:2 * npage]
    o_ref, p_sc, l_sc, acc_sc = rest[2 * npage:]
    s = pl.program_id(0)
    ppb = MOBA_BLOCK // page
    nblk = npage // ppb

    def heads(x):
        return jnp.stack([x[:, h * dh:(h + 1) * dh] for h in range(nhead)], axis=0)

    def pad_rows(x):
        return jnp.concatenate([x, jnp.zeros((page - t_new, x.shape[1]), x.dtype)], axis=0)

    @pl.when(s >= 1)
    def _():
        slot = (s + 1) % 2
        acc = acc_sc[slot]
        for pg in range(npage):
            acc = acc + jnp.einsum('htk,hkd->htd', p_sc[slot, pg].astype(BF16), vp[pg][0, 0].astype(BF16),
                                   preferred_element_type=F32)
        out = acc / l_sc[slot]
        for h in range(nhead):
            o_ref[:, h * dh:(h + 1) * dh] = out[h]

    @pl.when(s < bsz)
    def _():
        slot = s % 2
        q3 = heads(q_ref[...]).astype(BF16)
        lgs =[jnp.einsum('htd,hkd->htk', q3, kp[pg][0, 0].astype(BF16), preferred_element_type=F32)
               for pg in range(npage)]
        scores = []
        for b in range(nblk):
            tot = lgs[b * ppb]
            for r in range(1, ppb):
                tot = tot + lgs[b * ppb + r]
            scores.append(jnp.sum(tot, axis=-1, keepdims=True))
        sel = _topk_select(scores)
        far = bias_ref[:, 2]
        prev = bias_ref[:, 0]
        ml = []
        for pg in range(npage):
            b = pg // ppb
            r = pg % ppb
            bias = prev[:, :, r * page:(r + 1) * page] if b == nblk - 1 else far[:, :, :page]
            ml.append(jnp.where(sel[b], lgs[pg] + bias, NEG))
        kn3 = heads(pad_rows(kn_ref[...])).astype(BF16)
        vn3 = heads(pad_rows(vn_ref[...])).astype(BF16)
        lo = jnp.einsum('htd,hkd->htk', q3, kn3, preferred_element_type=F32) + bias_ref[:, 1][:, :, :page]
        m = jnp.max(lo, axis=-1, keepdims=True)
        for x in ml:
            m = jnp.maximum(m, jnp.max(x, axis=-1, keepdims=True))
        eo = jnp.exp(lo - m)
        l = jnp.sum(eo, axis=-1, keepdims=True)
        for pg in range(npage):
            e = jnp.exp(ml[pg] - m)
            l = l + jnp.sum(e, axis=-1, keepdims=True)
            p_sc[slot, pg] = e
        l_sc[slot] = l
        acc_sc[slot] = jnp.einsum('htk,hkd->htd', eo.astype(BF16), vn3, preferred_element_type=F32)


def _moba_sample(p, bias, kpool, vpool, layer, page_table, *, row_off, bsz, t_new, colq, colk, colv, nhead, dh):
    npage = page_table.shape[1]
    page = kpool.shape[3]
    assert dh == LANE and page == LANE and MOBA_BLOCK % page == 0
    assert (npage * page) % MOBA_BLOCK == 0 and npage * page >= MOBA_BLOCK and t_new <= page
    w = nhead * dh
    rb0 = row_off // t_new
    ntile = bias.shape[1]
    pt = page_table.reshape(-1).astype(jnp.int32)
    assert colq % nhead == 0 and colk % nhead == 0 and colv % nhead == 0
    colq, colk, colv = colq // nhead, colk // nhead, colv // nhead

    def cur(s):
        return jnp.minimum(s, bsz - 1)

    def prev(s):
        return jnp.maximum(s - 1, 0)

    def kmap(pg):
        return lambda s, pt_ref: (layer, pt_ref[cur(s) * npage + pg], 0, 0, 0)

    def vmap_(pg):
        return lambda s, pt_ref: (layer, pt_ref[prev(s) * npage + pg], 0, 0, 0)

    in_specs = [
        pl.BlockSpec((t_new, w), lambda s, pt_ref: (rb0 + cur(s), colq)),
        pl.BlockSpec((t_new, w), lambda s, pt_ref: (rb0 + cur(s), colk)),
        pl.BlockSpec((t_new, w), lambda s, pt_ref: (rb0 + cur(s), colv)),
        pl.BlockSpec((nhead, ntile, t_new, MOBA_BLOCK), lambda s, pt_ref: (0, 0, 0, 0)),
    ]
    in_specs += [pl.BlockSpec((1, 1, nhead, page, dh), kmap(pg)) for pg in range(npage)]
    in_specs += [pl.BlockSpec((1, 1, nhead, page, dh), vmap_(pg)) for pg in range(npage)]
    grid_spec = pltpu.PrefetchScalarGridSpec(
        num_scalar_prefetch=1,
        grid=(bsz + 1,),
        in_specs=in_specs,
        out_specs=pl.BlockSpec((t_new, w), lambda s, pt_ref: (prev(s), 0)),
        scratch_shapes=[
            pltpu.VMEM((2, npage, nhead, t_new, page), F32),
            pltpu.VMEM((2, nhead, t_new, 1), F32),
            pltpu.VMEM((2, nhead, t_new, dh), F32),
        ],
    )
    return pl.pallas_call(
        functools.partial(_moba_s_body, npage=npage, nhead=nhead, dh=dh, t_new=t_new, page=page, bsz=bsz),
        grid_spec=grid_spec,
        out_shape=jax.ShapeDtypeStruct((bsz * t_new, w), F32),
        compiler_params=_cparams(("arbitrary",)),
    )(pt, p, p, p, bias, *([kpool] * npage), *([vpool] * npage))


def _mix_body(*refs, starts):
    ns = len(starts) - 1
    hm_refs, hb_refs = refs[:ns], refs[ns:2 * ns]
    ga_ref, gb_ref, x_ref, wa_ref, wb_ref, wo_ref, o_ref = refs[2 * ns:]
    i = pl.program_id(0)
    a = _dot(_seg_load(hm_refs, starts, i).astype(BF16), wa_ref[...])
    b = _dot(_seg_load(hb_refs, starts, i).astype(BF16), wb_ref[...])
    merged = jax.nn.sigmoid(ga_ref[...].astype(F32)) * a + jax.nn.sigmoid(gb_ref[...].astype(F32)) * b
    o_ref[...] = x_ref[...] + _dot(merged.astype(BF16), wo_ref[...])


def _mix(hms, hbs, p, x, wa, wb, wo):
    m, d = x.shape
    wm = hms[0].shape[1]
    wbw = hbs[0].shape[1]
    rows = [h.shape[0] for h in hms]
    assert rows == [h.shape[0] for h in hbs] and sum(rows) == m
    tm = _pick(math.gcd(*rows), (256, 128, 64, 32, 16, 8))
    starts = _seg_starts(rows, tm)
    const = dict(pipeline_mode=pl.Buffered(1))
    return pl.pallas_call(
        functools.partial(_mix_body, starts=starts),
        grid=(m // tm,),
        in_specs=[_seg_spec(starts, k, tm, wm) for k in range(len(rows))]
        + [_seg_spec(starts, k, tm, wbw) for k in range(len(rows))]
        + [
            pl.BlockSpec((tm, d), lambda i: (i, 0)),
            pl.BlockSpec((tm, d), lambda i: (i, 1)),
            pl.BlockSpec((tm, d), lambda i: (i, 0)),
            pl.BlockSpec((wm, d), lambda i: (0, 0), **const),
            pl.BlockSpec((wbw, d), lambda i: (0, 0), **const),
            pl.BlockSpec((d, d), lambda i: (0, 0), **const),
        ],
        out_specs=pl.BlockSpec((tm, d), lambda i: (i, 0)),
        out_shape=jax.ShapeDtypeStruct((m, d), F32),
        compiler_params=_cparams(("parallel",)),
    )(*hms, *hbs, p, p, x, wa, wb, wo)


def kernel(x_prompt, x_sample, cache_k, cache_v, state_C, state_n, state_m, page_table, g_ff1, w_ff1_gate,
           w_ff1_up, w_ff1_down, g_mix, w_in, b_ig, b_fg, g_head, w_a, w_b, w_out, g_ff2, w_ff2_gate,
           w_ff2_up, w_ff2_down, rel_bias_table, g_final):
    bp, seq, d = x_prompt.shape
    bd, t_new, _ = x_sample.shape
    depth = g_ff1.shape[0]
    nh_m = b_ig.shape[1]
    w_m = w_a.shape[1]
    dh_m = w_m // nh_m
    nh_b = rel_bias_table.shape[1]
    w_bw = w_b.shape[1]
    dh_b = w_bw // nh_b
    page = cache_k.shape[3]
    mp = bp * seq
    sdt = state_C.dtype
    assert 2 * nh_m <= SUBLANE and (2 * d) % w_m == 0 and (2 * d + 4 * w_m) % LANE == 0
    assert mp % t_new == 0 and seq % MLSTM_PROMPT_CHUNK == 0 and seq % page == 0

    o_gate = 4 * w_m
    o_moba = o_gate + 2 * nh_m
    o_ga = o_moba + 3 * w_bw
    col_m = (2 * d) // w_m
    col_b = (2 * d + 4 * w_m) // dh_b
    colq_b, colk_b, colv_b = col_b, col_b + nh_b, col_b + 2 * nh_b

    n_slab = 2 * d + 4 * w_m + 3 * w_bw
    col_scale = np.ones((1, n_slab), np.float32)
    col_scale[:, 2 * d + w_m:2 * d + 2 * w_m] = dh_m ** -0.5
    col_scale[:, 2 * d + 4 * w_m:2 * d + 4 * w_m + w_bw] = dh_b ** -0.5
    col_scale = jnp.asarray(col_scale)

    rel_t = rel_bias_table.T.astype(F32)
    bias_p = _bias_tiles(rel_t, _prompt_bias_maps())
    bias_s = _bias_tiles(rel_t, _sample_bias_maps(t_new))

    ms_rows = bd * t_new
    xs = [x_prompt.reshape(mp, d), x_sample.reshape(ms_rows, d)]
    pages_like = jax.ShapeDtypeStruct((depth, mp // page, nh_b, page, dh_b), F32)
    rows_like = jax.ShapeDtypeStruct((depth, bd, nh_b, t_new, dh_b), F32)
    caches = (pages_like, pages_like, rows_like, rows_like)
    c_prompt = jax.ShapeDtypeStruct((depth, bp, nh_m, dh_m, dh_m), F32)
    c_sample = jax.ShapeDtypeStruct((depth, bd, nh_m, dh_m, dh_m), F32)
    np_l, mp_l, ns_l, ms_l = [], [], [], []
    for l in range(depth):
        wi = w_in[l]
        w_main = jnp.concatenate([wi[:, o_ga:], wi[:, :o_gate], wi[:, o_moba:o_ga]], axis=1).astype(BF16)
        w_gate = jnp.pad(wi[:, o_gate:o_moba], ((0, 0), (0, LANE - 2 * nh_m))).astype(BF16)
        gate_bias = jnp.concatenate([b_ig[l], b_fg[l]]).astype(F32)
        bias_c = jnp.pad(gate_bias, (0, LANE - 2 * nh_m)).reshape(1, LANE)
        bias_r = jnp.pad(gate_bias, (0, SUBLANE - 2 * nh_m)).reshape(SUBLANE, 1)

        x, hn = _ffn(xs, g_ff1[l], _cast_layer(w_ff1_gate, l), _cast_layer(w_ff1_up, l), _cast_layer(w_ff1_down, l),
                     g_next=g_mix[l])
        p, gates, gt_p, caches = _proj(hn, w_main, w_gate, col_scale, caches, l, mp=mp, nhead=nh_b,
                                       dh=dh_b, page=page, t_new=t_new)
        p_s = p[mp:].astype(F32)

        gt_s = gt_p[:, mp:].reshape(SUBLANE, bd, t_new)
        gt_s = jnp.pad(gt_s, ((0, 0), (0, 0), (0, LANE - t_new))).reshape(SUBLANE, bd * LANE)

        mk = dict(col0=col_m, nh=nh_m, dh=dh_m)
        hm_p, (c_prompt, npp, mpp) = _mlstm(p, gates, gt_p, bias_c, bias_r, g_head[l], c_prompt, l, row_off=0,
                                            bsz=bp, seq=seq, chunk=MLSTM_PROMPT_CHUNK, **mk)
        init = (state_C, state_n, state_m)
        hm_s, (c_sample, ns, ms) = _mlstm(p_s, gates[mp:], gt_s, bias_c, bias_r, g_head[l], c_sample, l, row_off=0,
                                          bsz=bd, seq=t_new, chunk=t_new, init=init, ne=math.gcd(bd, 4), **mk)

        bk = dict(colq=colq_b, colk=colk_b, colv=colv_b, nhead=nh_b, dh=dh_b)
        hb_p = _moba_prompt(rel_t, p, bias_p, bsz=bp, seq=seq, **bk)
        hb_s = _moba_sample(p_s, bias_s, cache_k, cache_v, l, page_table, row_off=0, bsz=bd, t_new=t_new, **bk)

        x = _mix([hm_p, hm_s], [hb_p, hb_s], p, x, _cast_layer(w_a, l), _cast_layer(w_b, l), _cast_layer(w_out, l))
        ffn2 = (g_ff2[l], _cast_layer(w_ff2_gate, l), _cast_layer(w_ff2_up, l), _cast_layer(w_ff2_down, l))
        if l == depth - 1:
            y_prompt, y_sample = _ffn([x], *ffn2, g_final=g_final, out_rows=(mp, ms_rows))
        else:
            xs = [_ffn([x], *ffn2)]

        np_l.append(npp.astype(sdt))
        mp_l.append(mpp.astype(sdt))
        ns_l.append(ns.astype(sdt))
        ms_l.append(ms.astype(sdt))

    k_pages, v_pages, k_rows, v_rows = caches
    page_shape = (depth, bp, seq // page, nh_b, page, dh_b)
    return (y_prompt.reshape(bp, seq, d), y_sample.reshape(bd, t_new, d),
            k_pages.reshape(page_shape), v_pages.reshape(page_shape), k_rows, v_rows,
            c_prompt.astype(sdt), jnp.stack(np_l), jnp.stack(mp_l), c_sample.astype(sdt), jnp.stack(ns_l), jnp.stack(ms_l))
```

```python
import functools
import math

import numpy as np
import jax
import jax.numpy as jnp
from jax import lax
from jax.experimental import pallas as pl
from jax.experimental.pallas import tpu as pltpu

F32 = jnp.float32
BF16 = jnp.bfloat16

NORM_EPS = 1e-6
MOBA_BLOCK = 256
MOBA_TOPK = 3
REL_BUCKETS = 32
REL_MAX_DIST = 128
MLSTM_PROMPT_CHUNK = 256
LANE = 128
SUBLANE = 8
NEG = -1e30
VMEM_LIMIT = 56 * 1024 * 1024


def _pick(n, cands):
    for c in cands:
        if n % c == 0:
            return c
    return n


def _cparams(sem):
    return pltpu.CompilerParams(dimension_semantics=sem, vmem_limit_bytes=VMEM_LIMIT)


def _rms(x, g):
    return x * lax.rsqrt(jnp.mean(x * x, axis=-1, keepdims=True) + NORM_EPS) * g


def _dot(a, b):
    return jnp.dot(a, b, preferred_element_type=F32)


def _dot_nt(a, b):
    return lax.dot_general(a, b, (((1,), (1,)), ((), ())), preferred_element_type=F32)


def _dot_tn(a, b):
    return lax.dot_general(a, b, (((0,), (0,)), ((), ())), preferred_element_type=F32)


CAST_BLOCK_BYTES = 6 * 1024 * 1024


def _cast_body(w_ref, o_ref):
    o_ref[...] = w_ref[0].astype(o_ref.dtype)


def _cast_layer(w, layer):
    _, rows, cols = w.shape
    tr = rows
    while tr % 2 == 0 and tr // 2 >= 2 * SUBLANE and tr * cols * 4 > CAST_BLOCK_BYTES:
        tr //= 2
    return pl.pallas_call(
        _cast_body,
        grid=(rows // tr,),
        in_specs=[pl.BlockSpec((1, tr, cols), lambda i: (layer, i, 0))],
        out_specs=pl.BlockSpec((tr, cols), lambda i: (i, 0)),
        out_shape=jax.ShapeDtypeStruct((rows, cols), BF16),
        compiler_params=_cparams(("parallel",)),
    )(w)


def _seg_starts(rows, tm):
    starts = [0]
    for r in rows:
        assert r % tm == 0
        starts.append(starts[-1] + r // tm)
    return starts


def _seg_spec(starts, k, tm, width, col=0, ngrid=1, **kw):
    lo, n = starts[k], starts[k + 1] - starts[k]
    if ngrid == 1:
        return pl.BlockSpec((tm, width), lambda i: (jnp.clip(i - lo, 0, n - 1), col), **kw)
    return pl.BlockSpec((tm, width), lambda i, j: (jnp.clip(i - lo, 0, n - 1), col), **kw)


def _seg_load(refs, starts, i):
    x = refs[0][...]
    for k in range(1, len(refs)):
        x = jnp.where(i >= starts[k], refs[k][...], x)
    return x


def _ffn_body(*refs, in_starts, out_starts, final, has_next):
    nx, no = len(in_starts) - 1, len(out_starts) - 1
    x_refs = refs[:nx]
    g_ref, wg_ref, wu_ref, wd_ref = refs[nx:nx + 4]
    rest = refs[nx + 4:]
    if final:
        gf_ref, rest = rest[0], rest[1:]
    if has_next:
        gn_ref, rest = rest[0], rest[1:]
    o_refs, rest = rest[:no], rest[no:]
    if has_next:
        hnext_ref, rest = rest[0], rest[1:]
    hn_ref = rest[0]
    acc_ref = rest[1] if no > 1 else o_refs[0]
    i = pl.program_id(0)
    f = pl.program_id(1)

    @pl.when(f == 0)
    def _():
        hn_ref[...] = _rms(_seg_load(x_refs, in_starts, i), g_ref[...]).astype(BF16)
        acc_ref[...] = jnp.zeros_like(acc_ref)

    h = hn_ref[...]
    g = _dot(h, wg_ref[...])
    u = _dot(h, wu_ref[...])
    a = (g * jax.nn.sigmoid(g) * u).astype(BF16)
    acc_ref[...] += _dot(a, wd_ref[...])

    @pl.when(f == pl.num_programs(1) - 1)
    def _():
        y = _seg_load(x_refs, in_starts, i) + 0.5 * acc_ref[...]
        if final:
            y = _rms(y, gf_ref[...])
        if has_next:
            hnext_ref[...] = _rms(y, gn_ref[...]).astype(BF16)
        if no == 1:
            o_refs[0][...] = y
        else:
            for k in range(no):
                def store(k=k):
                    o_refs[k][...] = y
                pl.when(jnp.logical_and(i >= out_starts[k], i < out_starts[k + 1]))(store)


def _ffn(xs, g, wg, wu, wd, g_final=None, out_rows=None, g_next=None):
    d = xs[0].shape[1]
    rows_in = [x.shape[0] for x in xs]
    m = sum(rows_in)
    rows_out = list(out_rows) if out_rows is not None else [m]
    ff = wg.shape[1]
    tm = _pick(math.gcd(*rows_in, *rows_out), (512, 256, 128, 64, 32, 16, 8))
    tf = _pick(ff, (512, 256, 128))
    final = g_final is not None
    in_starts = _seg_starts(rows_in, tm)
    out_starts = _seg_starts(rows_out, tm)
    in_specs = [_seg_spec(in_starts, k, tm, d, ngrid=2) for k in range(len(xs))]
    in_specs += [
        pl.BlockSpec((1, d), lambda i, f: (0, 0)),
        pl.BlockSpec((d, tf), lambda i, f: (0, f)),
        pl.BlockSpec((d, tf), lambda i, f: (0, f)),
        pl.BlockSpec((tf, d), lambda i, f: (f, 0)),
    ]
    args = [*xs, g.reshape(1, d), wg, wu, wd]
    if final:
        in_specs.append(pl.BlockSpec((1, d), lambda i, f: (0, 0)))
        args.append(g_final.reshape(1, d))
    has_next = g_next is not None
    out_specs = [_seg_spec(out_starts, k, tm, d, ngrid=2) for k in range(len(rows_out))]
    out_shape = [jax.ShapeDtypeStruct((r, d), F32) for r in rows_out]
    if has_next:
        assert out_rows is None
        in_specs.append(pl.BlockSpec((1, d), lambda i, f: (0, 0)))
        args.append(g_next.reshape(1, d))
        out_specs.append(pl.BlockSpec((tm, d), lambda i, f: (i, 0)))
        out_shape.append(jax.ShapeDtypeStruct((m, d), BF16))
    scratch = [pltpu.VMEM((tm, d), BF16)]
    if len(rows_out) > 1:
        scratch.append(pltpu.VMEM((tm, d), F32))
    outs = pl.pallas_call(
        functools.partial(_ffn_body, in_starts=in_starts, out_starts=out_starts, final=final, has_next=has_next),
        grid=(m // tm, ff // tf),
        in_specs=in_specs,
        out_specs=out_specs,
        out_shape=out_shape,
        scratch_shapes=scratch,
        compiler_params=_cparams(("parallel", "arbitrary")),
    )(*args)
    return outs if (out_rows is not None or has_next) else outs[0]


def _proj_body(hn_ref, w_ref, wgate_ref, cs_ref, *rest, j_k, j_v, npt, nhead, dh, page, t_new):
    p_ref, gate_ref, gatet_ref, kp_ref, vp_ref, ks_ref, vs_ref = rest[-7:]
    i = pl.program_id(0)
    j = pl.program_id(1)
    tm = hn_ref.shape[0]

    @pl.when(j == 0)
    def _():
        gate = _dot(hn_ref[...], wgate_ref[...])
        gate_ref[...] = gate
        gatet_ref[...] = gate.T[:SUBLANE, :]

    acc = _dot(hn_ref[...], w_ref[...])
    p_ref[...] = (acc * cs_ref[...]).astype(BF16)

    def put(pages_ref, rows_ref):
        @pl.when(i < npt)
        def _():
            for pg in range(tm // page):
                for h in range(nhead):
                    pages_ref[0, pg, h] = acc[pg * page:(pg + 1) * page, h * dh:(h + 1) * dh]

        @pl.when(i >= npt)
        def _():
            for h in range(nhead):
                rows_ref[0, :, h] = acc[:, h * dh:(h + 1) * dh].reshape(tm // t_new, t_new, dh)

    pl.when(j == j_k)(functools.partial(put, kp_ref, ks_ref))
    pl.when(j == j_v)(functools.partial(put, vp_ref, vs_ref))


def _proj(hn, w_main, w_gate, col_scale, caches, layer, *, mp, nhead, dh, page, t_new):
    carried = [] if isinstance(caches[0], jax.ShapeDtypeStruct) else list(caches)
    m, d = hn.shape
    n = w_main.shape[1]
    w = nhead * dh
    tn = w
    tm = _pick(math.gcd(mp, m - mp), (1024, 512, 256, 128))
    assert n % tn == 0 and tm % page == 0 and tm % t_new == 0
    j_k = (n - 2 * w) // tn
    npt = mp // tm
    nst = (m - mp) // tm
    once = dict(pipeline_mode=pl.Buffered(1))
    alias = pl.BlockSpec(memory_space=pl.ANY)
    pages_spec = pl.BlockSpec((1, tm // page, nhead, page, dh),
                              lambda i, j: (layer, jnp.minimum(i, npt - 1), 0, 0, 0), **once)
    rows_spec = pl.BlockSpec((1, tm // t_new, nhead, t_new, dh),
                             lambda i, j: (layer, jnp.clip(i - npt, 0, nst - 1), 0, 0, 0), **once)
    outs = pl.pallas_call(
        functools.partial(_proj_body, j_k=j_k, j_v=j_k + 1, npt=npt, nhead=nhead, dh=dh, page=page, t_new=t_new),
        grid=(m // tm, n // tn),
        in_specs=[
            pl.BlockSpec((tm, d), lambda i, j: (i, 0)),
            pl.BlockSpec((d, tn), lambda i, j: (0, j)),
            pl.BlockSpec((d, LANE), lambda i, j: (0, 0)),
            pl.BlockSpec((1, tn), lambda i, j: (0, j)),
        ] + [alias] * len(carried),
        out_specs=[
            pl.BlockSpec((tm, tn), lambda i, j: (i, j)),
            pl.BlockSpec((tm, LANE), lambda i, j: (i, 0)),
            pl.BlockSpec((SUBLANE, tm), lambda i, j: (0, i)),
            pages_spec, pages_spec, rows_spec, rows_spec,
        ],
        out_shape=[jax.ShapeDtypeStruct((m, n), BF16), jax.ShapeDtypeStruct((m, LANE), F32),
                   jax.ShapeDtypeStruct((SUBLANE, m), F32)]
        + [jax.ShapeDtypeStruct(c.shape, c.dtype) for c in caches],
        input_output_aliases={4 + k: 3 + k for k in range(len(carried))},
        compiler_params=_cparams(("parallel", "arbitrary")),
    )(hn, w_main, w_gate, col_scale, *carried)
    return outs[0], outs[1], outs[2], tuple(outs[3:])


def _log_sigmoid(x):
    return jnp.minimum(x, 0.0) - jnp.log(1.0 + jnp.exp(-jnp.abs(x)))


def _mlstm_body(*refs, nh, dh, lq, lk, ne, has_init):
    q_ref, k_ref, v_ref, o_ref, gc_ref, gr_ref, bc_ref, br_ref, gh_ref = refs[:9]
    if has_init:
        c0_ref, n0_ref, m0_ref = refs[9:12]
    h_ref, cn_ref, nn_ref, mn_ref, c_sc, n_sc, m_sc = refs[-7:]
    c = pl.program_id(1)
    last = pl.num_programs(1) - 1

    @pl.when(c == 0)
    def _():
        if has_init:
            c_sc[...] = c0_ref[0].astype(F32)
            n_sc[...] = n0_ref[0].astype(F32)
            m_sc[...] = m0_ref[0].astype(F32)
        else:
            c_sc[...] = jnp.zeros_like(c_sc)
            n_sc[...] = jnp.zeros_like(n_sc)
            m_sc[...] = jnp.zeros_like(m_sc)

    for e in range(ne):
        rows = slice(e * lq, (e + 1) * lq)
        gcol = gc_ref[rows, :] + bc_ref[...]
        grow = gr_ref[:, e * lk:(e + 1) * lk] + br_ref[...]
        lf_col = _log_sigmoid(gcol)
        lf_row = _log_sigmoid(grow)
        if lq >= LANE:
            ri = lax.broadcasted_iota(jnp.int32, (lq, lq), 0)
            ci = lax.broadcasted_iota(jnp.int32, (lq, lq), 1)
            tril = jnp.where(ci <= ri, 1.0, 0.0).astype(F32)
            triu = jnp.where(ri <= ci, 1.0, 0.0).astype(F32)
            b_col = jnp.dot(tril, lf_col, precision=lax.Precision.HIGHEST, preferred_element_type=F32)
            b_row = jnp.dot(lf_row, triu, precision=lax.Precision.HIGHEST, preferred_element_type=F32)
        else:
            ri = lax.broadcasted_iota(jnp.int32, (lq, LANE), 0)
            ci = lax.broadcasted_iota(jnp.int32, (SUBLANE, lk), 1)
            b_col = jnp.zeros((lq, LANE), F32)
            b_row = jnp.zeros((SUBLANE, lk), F32)
            for s in range(lq):
                b_col = b_col + jnp.where(ri >= s, lf_col[s:s + 1, :], 0.0)
                b_row = b_row + jnp.where(ci >= s, lf_row[:, s:s + 1], 0.0)

        qi = lax.broadcasted_iota(jnp.int32, (lq, lk), 0)
        ki = lax.broadcasted_iota(jnp.int32, (lq, lk), 1)
        causal = ki <= qi
        kvalid = lax.broadcasted_iota(jnp.int32, (lk, 1), 0) < lq
        q_all = q_ref[rows, :]
        k_all = k_ref[rows, :]
        v_all = v_ref[rows, :]
        o_all = o_ref[rows, :]
        gh_all = gh_ref[...]
        m_prev_all = m_sc[e]

        def pad_keys(a):
            if lk == lq:
                return a
            return jnp.concatenate([a, jnp.zeros((lk - lq, a.shape[1]), a.dtype)], axis=0)

        for h in range(nh):
            sl = slice(h * dh, (h + 1) * dh)
            qh = q_all[:, sl].astype(F32)
            kh = pad_keys(k_all[:, sl]).astype(F32)
            vh = pad_keys(v_all[:, sl]).astype(F32)
            bq = b_col[:, nh + h:nh + h + 1]
            ig_k = pad_keys(gcol[:, h:h + 1])
            bk = pad_keys(bq)
            r_row = grow[h:h + 1, :] - b_row[nh + h:nh + h + 1, :]
            m_prev = m_prev_all[:, h:h + 1]
            log_d = jnp.where(causal, bq + r_row, -jnp.inf)
            log_p = bq + m_prev
            m_t = jnp.maximum(log_p, jnp.max(log_d, axis=-1, keepdims=True))
            w_intra = jnp.exp(log_d - m_t)
            w_prev = jnp.exp(log_p - m_t)
            qb = qh.astype(BF16)
            kb = kh.astype(BF16)
            s = _dot_nt(qb, kb) * w_intra
            cmat = c_sc[e, h]
            nrow = n_sc[e, h:h + 1, :]
            num = _dot(s.astype(BF16), vh.astype(BF16)) + w_prev * _dot_nt(qb, cmat.astype(BF16))
            den = jnp.sum(s, axis=-1, keepdims=True) + w_prev * jnp.sum(qh * nrow, axis=-1, keepdims=True)
            hh = num / jnp.maximum(jnp.abs(den), jnp.exp(-m_t))
            b_last = bq[lq - 1:lq, :]
            m_last = m_t[lq - 1:lq, :]
            w_last = jnp.where(kvalid, jnp.exp(b_last - bk + ig_k - m_last), 0.0)
            decay = w_prev[lq - 1:lq, :]
            c_new = decay * cmat + _dot_tn((vh * w_last).astype(BF16), kb)
            n_new = decay * nrow + jnp.sum(kh * w_last, axis=0, keepdims=True)
            c_sc[e, h] = c_new
            n_sc[e, h:h + 1, :] = n_new
            m_sc[e, :, h:h + 1] = m_last
            hm = jax.nn.sigmoid(o_all[:, sl].astype(F32)) * hh
            h_ref[rows, sl] = _rms(hm, gh_all[:, sl])

    @pl.when(c == last)
    def _():
        cn_ref[0] = c_sc[...]
        nn_ref[...] = n_sc[...]
        mn_ref[...] = m_sc[...]


def _mlstm(p, gates, gates_t, bias_c, bias_r, g_head, c_stack, layer, *, row_off, bsz, seq, chunk, col0, nh, dh,
           init=None, ne=1):
    w = nh * dh
    lq = chunk
    lk = max(chunk, LANE)
    nc = seq // chunk
    assert ne == 1 or (nc == 1 and bsz % ne == 0)
    rb0 = row_off // (ne * lq)
    has_init = init is not None

    def rowblk(b, c):
        return rb0 + b * nc + c

    in_specs = [
        pl.BlockSpec((ne * lq, w), lambda b, c: (rowblk(b, c), col0)),
        pl.BlockSpec((ne * lq, w), lambda b, c: (rowblk(b, c), col0 + 1)),
        pl.BlockSpec((ne * lq, w), lambda b, c: (rowblk(b, c), col0 + 2)),
        pl.BlockSpec((ne * lq, w), lambda b, c: (rowblk(b, c), col0 + 3)),
        pl.BlockSpec((ne * lq, LANE), lambda b, c: (rowblk(b, c), 0)),
        pl.BlockSpec((SUBLANE, ne * lk), lambda b, c: (0, rowblk(b, c))),
        pl.BlockSpec((1, LANE), lambda b, c: (0, 0)),
        pl.BlockSpec((SUBLANE, 1), lambda b, c: (0, 0)),
        pl.BlockSpec((1, w), lambda b, c: (0, 0)),
    ]
    args = [p, p, p, p, gates, gates_t, bias_c, bias_r, g_head.reshape(1, w)]
    if has_init:
        c0, n0, m0 = init
        in_specs += [
            pl.BlockSpec((1, ne, nh, dh, dh), lambda b, c: (layer, b, 0, 0, 0)),
            pl.BlockSpec((1, ne, nh, dh), lambda b, c: (layer, b, 0, 0)),
            pl.BlockSpec((1, ne, 1, nh), lambda b, c: (layer, b, 0, 0)),
        ]
        args += [c0, n0, m0.reshape(m0.shape[0], bsz, 1, nh)]
    carried = not isinstance(c_stack, jax.ShapeDtypeStruct)
    if carried:
        in_specs.append(pl.BlockSpec(memory_space=pl.ANY))
        args.append(c_stack)
    out_specs = [
        pl.BlockSpec((ne * lq, w), lambda b, c: (b * nc + c, 0)),
        pl.BlockSpec((1, ne, nh, dh, dh), lambda b, c: (layer, b, 0, 0, 0)),
        pl.BlockSpec((ne, nh, dh), lambda b, c: (b, 0, 0)),
        pl.BlockSpec((ne, 1, nh), lambda b, c: (b, 0, 0)),
    ]
    out_shape = [
        jax.ShapeDtypeStruct((bsz * seq, w), F32),
        jax.ShapeDtypeStruct(c_stack.shape, F32),
        jax.ShapeDtypeStruct((bsz, nh, dh), F32),
        jax.ShapeDtypeStruct((bsz, 1, nh), F32),
    ]
    h, cn, nn, mn = pl.pallas_call(
        functools.partial(_mlstm_body, nh=nh, dh=dh, lq=lq, lk=lk, ne=ne, has_init=has_init),
        grid=(bsz // ne, nc),
        in_specs=in_specs,
        out_specs=out_specs,
        out_shape=out_shape,
        input_output_aliases={len(args) - 1: 1} if carried else {},
        scratch_shapes=[pltpu.VMEM((ne, nh, dh, dh), F32), pltpu.VMEM((ne, nh, dh), F32),
                        pltpu.VMEM((ne, 1, nh), F32)],
        compiler_params=_cparams(("parallel", "arbitrary")),
    )(*args)
    return h, (cn, nn, mn.reshape(bsz, nh))


def _bucket_np(dist):
    n = np.maximum(dist, 0)
    max_exact = REL_BUCKETS // 2
    nf = np.maximum(n, 1).astype(np.float32)
    large = max_exact + (np.log(nf / np.float32(max_exact)) / np.float32(math.log(REL_MAX_DIST / max_exact))
                         * np.float32(REL_BUCKETS - max_exact)).astype(np.int32)
    large = np.minimum(large, REL_BUCKETS - 1)
    return np.where(dist < 0, -1, np.where(n < max_exact, n, large)).astype(np.int32)


def _bias_body(tbl_ref, map_ref, o_ref, *, ntile):
    h = pl.program_id(0)
    for t in range(ntile):
        bm = map_ref[t]
        acc = jnp.full(bm.shape, NEG, F32)
        for b in range(REL_BUCKETS):
            acc = jnp.where(bm == b, tbl_ref[h, b], acc)
        o_ref[0, t] = acc


def _bias_tiles(rel_t, maps):
    nhead = rel_t.shape[0]
    ntile, r, c = maps.shape
    return pl.pallas_call(
        functools.partial(_bias_body, ntile=ntile),
        grid=(nhead,),
        in_specs=[
            pl.BlockSpec(memory_space=pltpu.SMEM),
            pl.BlockSpec((ntile, r, c), lambda h: (0, 0, 0)),
        ],
        out_specs=pl.BlockSpec((1, ntile, r, c), lambda h: (h, 0, 0, 0)),
        out_shape=jax.ShapeDtypeStruct((nhead, ntile, r, c), F32),
        compiler_params=_cparams(("arbitrary",)),
    )(rel_t, jnp.asarray(maps))


def _prompt_bias_maps():
    k = np.arange(MOBA_BLOCK)[:, None]
    q = np.arange(MOBA_BLOCK)[None, :]
    return np.stack([_bucket_np(q - k), _bucket_np(MOBA_BLOCK + q - k)])


def _sample_bias_maps(t_new):
    r = np.arange(t_new)[:, None]
    c = np.arange(MOBA_BLOCK)[None, :]
    own = np.where(c < t_new, r - c, -1)
    return np.stack([_bucket_np(MOBA_BLOCK + r - c), _bucket_np(own), _bucket_np(2 * MOBA_BLOCK + r - c + t_new)])


def _topk_select(scores):
    n = len(scores)
    sel = []
    for a in range(n):
        cnt = jnp.zeros(scores[a].shape, F32)
        for b in range(n):
            if b == a:
                continue
            ahead = (scores[b] >= scores[a]) if b < a else (scores[b] > scores[a])
            cnt = cnt + jnp.where(ahead, 1.0, 0.0)
        keep = cnt < MOBA_TOPK
        sel.append(keep)
    return sel


def _moba_p_body(tbl_ref, q_ref, k_ref, v_ref, bias_ref, o_ref, km_sc, vt_sc, lg_sc, *, nblk, dh, hps):
    hg = pl.program_id(1)
    j = pl.program_id(2)
    blk = MOBA_BLOCK

    @pl.when(j == 0)
    def _():
        km_sc[...] = jnp.zeros_like(km_sc)
        for hh in range(hps):
            sl = slice(hh * dh, (hh + 1) * dh)
            for b in range(nblk):
                kblk = k_ref[b * blk:(b + 1) * blk, sl].astype(F32)
                km_sc[hh, b:b + 1, :] = jnp.mean(kblk, axis=0, keepdims=True)
            vt_sc[hh] = v_ref[:, sl].astype(F32).T.astype(BF16)

    def select(hh):
        qs = q_ref[:, hh * dh:(hh + 1) * dh]
        st = _dot_nt(km_sc[hh].astype(BF16), qs)
        rowi = lax.broadcasted_iota(jnp.int32, st.shape, 0)
        st = jnp.where(rowi < j, st, -jnp.inf)
        cnt = jnp.zeros(st.shape, F32)
        for b in range(nblk):
            sb = st[b:b + 1, :]
            cnt = cnt + jnp.where(rowi > b, jnp.where(sb >= st, 1.0, 0.0), jnp.where(sb > st, 1.0, 0.0))
        return qs, jnp.where(cnt < MOBA_TOPK, jnp.where(rowi < j, 1.0, 0.0), 0.0)

    picked = [select(hh) for hh in range(hps)]

    def run(jj):
        for hh in range(hps):
            sl = slice(hh * dh, (hh + 1) * dh)
            qs, sel_t = picked[hh]
            far = tbl_ref[hg * hps + hh, REL_BUCKETS - 1]
            m = None
            for b in range(jj + 1):
                lg = _dot_nt(k_ref[b * blk:(b + 1) * blk, sl], qs)
                if b == jj:
                    lg = lg + bias_ref[hh, 0]
                elif b == jj - 1:
                    lg = lg + bias_ref[hh, 1]
                else:
                    lg = lg + far
                if b < jj:
                    lg = jnp.where(sel_t[b:b + 1, :] > 0.5, lg, NEG)
                lg_sc[hh, b] = lg
                tmax = jnp.max(lg, axis=0, keepdims=True)
                m = tmax if m is None else jnp.maximum(m, tmax)
            l = jnp.zeros_like(m)
            acc = jnp.zeros((dh, blk), F32)
            for b in range(jj + 1):
                pr = jnp.exp(lg_sc[hh, b] - m)
                l = l + jnp.sum(pr, axis=0, keepdims=True)
                acc = acc + _dot(vt_sc[hh, :, b * blk:(b + 1) * blk], pr.astype(BF16))
            o_ref[:, sl] = (acc / l).T

    for jj in range(nblk):
        pl.when(j == jj)(functools.partial(run, jj))


def _moba_prompt(rel_t, p, bias, *, bsz, seq, colq, colk, colv, nhead, dh):
    assert dh == LANE and seq % MOBA_BLOCK == 0 and MOBA_BLOCK >= REL_MAX_DIST
    nblk = seq // MOBA_BLOCK
    assert nblk <= SUBLANE
    ntile = bias.shape[1]
    hps = 4 if nhead % 4 == 0 and colq % 4 == 0 and colk % 4 == 0 and colv % 4 == 0 else 2
    assert nhead % hps == 0 and colq % hps == 0 and colk % hps == 0 and colv % hps == 0
    cq, ck, cv = colq // hps, colk // hps, colv // hps
    return pl.pallas_call(
        functools.partial(_moba_p_body, nblk=nblk, dh=dh, hps=hps),
        grid=(bsz, nhead // hps, nblk),
        in_specs=[
            pl.BlockSpec(memory_space=pltpu.SMEM),
            pl.BlockSpec((MOBA_BLOCK, hps * dh), lambda b, h, j: (b * nblk + j, cq + h)),
            pl.BlockSpec((seq, hps * dh), lambda b, h, j: (b, ck + h)),
            pl.BlockSpec((seq, hps * dh), lambda b, h, j: (b, cv + h)),
            pl.BlockSpec((hps, ntile, MOBA_BLOCK, MOBA_BLOCK), lambda b, h, j: (h, 0, 0, 0)),
        ],
        out_specs=pl.BlockSpec((MOBA_BLOCK, hps * dh), lambda b, h, j: (b * nblk + j, h)),
        out_shape=jax.ShapeDtypeStruct((bsz * seq, nhead * dh), F32),
        scratch_shapes=[
            pltpu.VMEM((hps, 2 * SUBLANE, dh), F32),
            pltpu.VMEM((hps, dh, seq), BF16),
            pltpu.VMEM((hps, nblk, MOBA_BLOCK, MOBA_BLOCK), F32),
        ],
        compiler_params=_cparams(("parallel", "parallel", "arbitrary")),
    )(rel_t, p, p, p, bias)


def _moba_s_body(pt_ref, q_ref, kn_ref, vn_ref, bias_ref, *rest, npage, nhead, dh, t_new, page, bsz):
    kp = rest[:npage]
    vp = rest[npage:2 * npage]
    o_ref, p_sc, l_sc, acc_sc = rest[2 * npage:]
    s = pl.program_id(0)
    ppb = MOBA_BLOCK // page
    nblk = npage // ppb

    def heads(x):
        return jnp.stack([x[:, h * dh:(h + 1) * dh] for h in range(nhead)], axis=0)

    def pad_rows(x):
        return jnp.concatenate([x, jnp.zeros((page - t_new, x.shape[1]), x.dtype)], axis=0)

    @pl.when(s >= 1)
    def _():
        slot = (s + 1) % 2
        acc = acc_sc[slot]
        for pg in range(npage):
            acc = acc + jnp.einsum('htk,hkd->htd', p_sc[slot, pg].astype(BF16), vp[pg][0, 0].astype(BF16),
                                   preferred_element_type=F32)
        out = acc / l_sc[slot]
        for h in range(nhead):
            o_ref[:, h * dh:(h + 1) * dh] = out[h]

    @pl.when(s < bsz)
    def _():
        slot = s % 2
        q3 = heads(q_ref[...]).astype(BF16)
        lgs =[jnp.einsum('htd,hkd->htk', q3, kp[pg][0, 0].astype(BF16), preferred_element_type=F32)
               for pg in range(npage)]
        scores = []
        for b in range(nblk):
            tot = lgs[b * ppb]
            for r in range(1, ppb):
                tot = tot + lgs[b * ppb + r]
            scores.append(jnp.sum(tot, axis=-1, keepdims=True))
        sel = _topk_select(scores)
        far = bias_ref[:, 2]
        prev = bias_ref[:, 0]
        ml = []
        for pg in range(npage):
            b = pg // ppb
            r = pg % ppb
            bias = prev[:, :, r * page:(r + 1) * page] if b == nblk - 1 else far[:, :, :page]
            ml.append(jnp.where(sel[b], lgs[pg] + bias, NEG))
        kn3 = heads(pad_rows(kn_ref[...])).astype(BF16)
        vn3 = heads(pad_rows(vn_ref[...])).astype(BF16)
        lo = jnp.einsum('htd,hkd->htk', q3, kn3, preferred_element_type=F32) + bias_ref[:, 1][:, :, :page]
        m = jnp.max(lo, axis=-1, keepdims=True)
        for x in ml:
            m = jnp.maximum(m, jnp.max(x, axis=-1, keepdims=True))
        eo = jnp.exp(lo - m)
        l = jnp.sum(eo, axis=-1, keepdims=True)
        for pg in range(npage):
            e = jnp.exp(ml[pg] - m)
            l = l + jnp.sum(e, axis=-1, keepdims=True)
            p_sc[slot, pg] = e
        l_sc[slot] = l
        acc_sc[slot] = jnp.einsum('htk,hkd->htd', eo.astype(BF16), vn3, preferred_element_type=F32)


def _moba_sample(p, bias, kpool, vpool, layer, page_table, *, row_off, bsz, t_new, colq, colk, colv, nhead, dh):
    npage = page_table.shape[1]
    page = kpool.shape[3]
    assert dh == LANE and page == LANE and MOBA_BLOCK % page == 0
    assert (npage * page) % MOBA_BLOCK == 0 and npage * page >= MOBA_BLOCK and t_new <= page
    w = nhead * dh
    rb0 = row_off // t_new
    ntile = bias.shape[1]
    pt = page_table.reshape(-1).astype(jnp.int32)
    assert colq % nhead == 0 and colk % nhead == 0 and colv % nhead == 0
    colq, colk, colv = colq // nhead, colk // nhead, colv // nhead

    def cur(s):
        return jnp.minimum(s, bsz - 1)

    def prev(s):
        return jnp.maximum(s - 1, 0)

    def kmap(pg):
        return lambda s, pt_ref: (layer, pt_ref[cur(s) * npage + pg], 0, 0, 0)

    def vmap_(pg):
        return lambda s, pt_ref: (layer, pt_ref[prev(s) * npage + pg], 0, 0, 0)

    in_specs = [
        pl.BlockSpec((t_new, w), lambda s, pt_ref: (rb0 + cur(s), colq)),
        pl.BlockSpec((t_new, w), lambda s, pt_ref: (rb0 + cur(s), colk)),
        pl.BlockSpec((t_new, w), lambda s, pt_ref: (rb0 + cur(s), colv)),
        pl.BlockSpec((nhead, ntile, t_new, MOBA_BLOCK), lambda s, pt_ref: (0, 0, 0, 0)),
    ]
    in_specs += [pl.BlockSpec((1, 1, nhead, page, dh), kmap(pg)) for pg in range(npage)]
    in_specs += [pl.BlockSpec((1, 1, nhead, page, dh), vmap_(pg)) for pg in range(npage)]
    grid_spec = pltpu.PrefetchScalarGridSpec(
        num_scalar_prefetch=1,
        grid=(bsz + 1,),
        in_specs=in_specs,
        out_specs=pl.BlockSpec((t_new, w), lambda s, pt_ref: (prev(s), 0)),
        scratch_shapes=[
            pltpu.VMEM((2, npage, nhead, t_new, page), F32),
            pltpu.VMEM((2, nhead, t_new, 1), F32),
            pltpu.VMEM((2, nhead, t_new, dh), F32),
        ],
    )
    return pl.pallas_call(
        functools.partial(_moba_s_body, npage=npage, nhead=nhead, dh=dh, t_new=t_new, page=page, bsz=bsz),
        grid_spec=grid_spec,
        out_shape=jax.ShapeDtypeStruct((bsz * t_new, w), F32),
        compiler_params=_cparams(("arbitrary",)),
    )(pt, p, p, p, bias, *([kpool] * npage), *([vpool] * npage))


def _mix_body(*refs, starts):
    ns = len(starts) - 1
    hm_refs, hb_refs = refs[:ns], refs[ns:2 * ns]
    ga_ref, gb_ref, x_ref, wa_ref, wb_ref, wo_ref, o_ref = refs[2 * ns:]
    i = pl.program_id(0)
    a = _dot(_seg_load(hm_refs, starts, i).astype(BF16), wa_ref[...])
    b = _dot(_seg_load(hb_refs, starts, i).astype(BF16), wb_ref[...])
    merged = jax.nn.sigmoid(ga_ref[...].astype(F32)) * a + jax.nn.sigmoid(gb_ref[...].astype(F32)) * b
    o_ref[...] = x_ref[...] + _dot(merged.astype(BF16), wo_ref[...])


def _mix(hms, hbs, p, x, wa, wb, wo):
    m, d = x.shape
    wm = hms[0].shape[1]
    wbw = hbs[0].shape[1]
    rows = [h.shape[0] for h in hms]
    assert rows == [h.shape[0] for h in hbs] and sum(rows) == m
    tm = _pick(math.gcd(*rows), (256, 128, 64, 32, 16, 8))
    starts = _seg_starts(rows, tm)
    const = dict(pipeline_mode=pl.Buffered(1))
    return pl.pallas_call(
        functools.partial(_mix_body, starts=starts),
        grid=(m // tm,),
        in_specs=[_seg_spec(starts, k, tm, wm) for k in range(len(rows))]
        + [_seg_spec(starts, k, tm, wbw) for k in range(len(rows))]
        + [
            pl.BlockSpec((tm, d), lambda i: (i, 0)),
            pl.BlockSpec((tm, d), lambda i: (i, 1)),
            pl.BlockSpec((tm, d), lambda i: (i, 0)),
            pl.BlockSpec((wm, d), lambda i: (0, 0), **const),
            pl.BlockSpec((wbw, d), lambda i: (0, 0), **const),
            pl.BlockSpec((d, d), lambda i: (0, 0), **const),
        ],
        out_specs=pl.BlockSpec((tm, d), lambda i: (i, 0)),
        out_shape=jax.ShapeDtypeStruct((m, d), F32),
        compiler_params=_cparams(("parallel",)),
    )(*hms, *hbs, p, p, x, wa, wb, wo)


def kernel(x_prompt, x_sample, cache_k, cache_v, state_C, state_n, state_m, page_table, g_ff1, w_ff1_gate,
           w_ff1_up, w_ff1_down, g_mix, w_in, b_ig, b_fg, g_head, w_a, w_b, w_out, g_ff2, w_ff2_gate,
           w_ff2_up, w_ff2_down, rel_bias_table, g_final):
    bp, seq, d = x_prompt.shape
    bd, t_new, _ = x_sample.shape
    depth = g_ff1.shape[0]
    nh_m = b_ig.shape[1]
    w_m = w_a.shape[1]
    dh_m = w_m // nh_m
    nh_b = rel_bias_table.shape[1]
    w_bw = w_b.shape[1]
    dh_b = w_bw // nh_b
    page = cache_k.shape[3]
    mp = bp * seq
    sdt = state_C.dtype
    assert 2 * nh_m <= SUBLANE and (2 * d) % w_m == 0 and (2 * d + 4 * w_m) % LANE == 0
    assert mp % t_new == 0 and seq % MLSTM_PROMPT_CHUNK == 0 and seq % page == 0

    o_gate = 4 * w_m
    o_moba = o_gate + 2 * nh_m
    o_ga = o_moba + 3 * w_bw
    col_m = (2 * d) // w_m
    col_b = (2 * d + 4 * w_m) // dh_b
    colq_b, colk_b, colv_b = col_b, col_b + nh_b, col_b + 2 * nh_b

    n_slab = 2 * d + 4 * w_m + 3 * w_bw
    col_scale = np.ones((1, n_slab), np.float32)
    col_scale[:, 2 * d + w_m:2 * d + 2 * w_m] = dh_m ** -0.5
    col_scale[:, 2 * d + 4 * w_m:2 * d + 4 * w_m + w_bw] = dh_b ** -0.5
    col_scale = jnp.asarray(col_scale)

    rel_t = rel_bias_table.T.astype(F32)
    bias_p = _bias_tiles(rel_t, _prompt_bias_maps())
    bias_s = _bias_tiles(rel_t, _sample_bias_maps(t_new))

    ms_rows = bd * t_new
    xs = [x_prompt.reshape(mp, d), x_sample.reshape(ms_rows, d)]
    pages_like = jax.ShapeDtypeStruct((depth, mp // page, nh_b, page, dh_b), F32)
    rows_like = jax.ShapeDtypeStruct((depth, bd, nh_b, t_new, dh_b), F32)
    caches = (pages_like, pages_like, rows_like, rows_like)
    c_prompt = jax.ShapeDtypeStruct((depth, bp, nh_m, dh_m, dh_m), F32)
    c_sample = jax.ShapeDtypeStruct((depth, bd, nh_m, dh_m, dh_m), F32)
    np_l, mp_l, ns_l, ms_l = [], [], [], []
    for l in range(depth):
        wi = w_in[l]
        w_main = jnp.concatenate([wi[:, o_ga:], wi[:, :o_gate], wi[:, o_moba:o_ga]], axis=1).astype(BF16)
        w_gate = jnp.pad(wi[:, o_gate:o_moba], ((0, 0), (0, LANE - 2 * nh_m))).astype(BF16)
        gate_bias = jnp.concatenate([b_ig[l], b_fg[l]]).astype(F32)
        bias_c = jnp.pad(gate_bias, (0, LANE - 2 * nh_m)).reshape(1, LANE)
        bias_r = jnp.pad(gate_bias, (0, SUBLANE - 2 * nh_m)).reshape(SUBLANE, 1)

        x, hn = _ffn(xs, g_ff1[l], _cast_layer(w_ff1_gate, l), _cast_layer(w_ff1_up, l), _cast_layer(w_ff1_down, l),
                     g_next=g_mix[l])
        p, gates, gt_p, caches = _proj(hn, w_main, w_gate, col_scale, caches, l, mp=mp, nhead=nh_b,
                                       dh=dh_b, page=page, t_new=t_new)
        p_s = p[mp:].astype(F32)

        gt_s = gt_p[:, mp:].reshape(SUBLANE, bd, t_new)
        gt_s = jnp.pad(gt_s, ((0, 0), (0, 0), (0, LANE - t_new))).reshape(SUBLANE, bd * LANE)

        mk = dict(col0=col_m, nh=nh_m, dh=dh_m)
        hm_p, (c_prompt, npp, mpp) = _mlstm(p, gates, gt_p, bias_c, bias_r, g_head[l], c_prompt, l, row_off=0,
                                            bsz=bp, seq=seq, chunk=MLSTM_PROMPT_CHUNK, **mk)
        init = (state_C, state_n, state_m)
        hm_s, (c_sample, ns, ms) = _mlstm(p_s, gates[mp:], gt_s, bias_c, bias_r, g_head[l], c_sample, l, row_off=0,
                                          bsz=bd, seq=t_new, chunk=t_new, init=init, ne=math.gcd(bd, 4), **mk)

        bk = dict(colq=colq_b, colk=colk_b, colv=colv_b, nhead=nh_b, dh=dh_b)
        hb_p = _moba_prompt(rel_t, p, bias_p, bsz=bp, seq=seq, **bk)
        hb_s = _moba_sample(p_s, bias_s, cache_k, cache_v, l, page_table, row_off=0, bsz=bd, t_new=t_new, **bk)

        x = _mix([hm_p, hm_s], [hb_p, hb_s], p, x, _cast_layer(w_a, l), _cast_layer(w_b, l), _cast_layer(w_out, l))
        ffn2 = (g_ff2[l], _cast_layer(w_ff2_gate, l), _cast_layer(w_ff2_up, l), _cast_layer(w_ff2_down, l))
        if l == depth - 1:
            y_prompt, y_sample = _ffn([x], *ffn2, g_final=g_final, out_rows=(mp, ms_rows))
        else:
            xs = [_ffn([x], *ffn2)]

        np_l.append(npp.astype(sdt))
        mp_l.append(mpp.astype(sdt))
        ns_l.append(ns.astype(sdt))
        ms_l.append(ms.astype(sdt))

    k_pages, v_pages, k_rows, v_rows = caches
    page_shape = (depth, bp, seq // page, nh_b, page, dh_b)
    return (y_prompt.reshape(bp, seq, d), y_sample.reshape(bd, t_new, d),
            k_pages.reshape(page_shape), v_pages.reshape(page_shape), k_rows, v_rows,
            c_prompt.astype(sdt), jnp.stack(np_l), jnp.stack(mp_l), c_sample.astype(sdt), jnp.stack(ns_l), jnp.stack(ms_l))
```
